```python
import math
import jax, jax.numpy as jnp
from jax import lax
import numpy as np

D_MODEL = 1024
BATCH = 8
SEQ = 2048
DEPTH = 4
DEC_BATCH = 128
DEC_SEQ = 8
PAST_LEN = 2048
PAGE_SIZE = 128

N_A = (DEPTH + 1) // 2
N_C = DEPTH // 2
HD = 64
H_A = (D_MODEL // 2) // HD
KVH = 2
HPG = H_A // KVH
L_CMP = 32
L_SEL = 64
N_SEL = 8
WINDOW = 512
ROPE_THETA = 10000.0
Q_BLOCK = 128
FORCE_BONUS = 1.0e4
NEG_INF = -1.0e30
TINY = 1.0e-30
C_B = D_MODEL // 2
CONV_W = 31
D_C = D_MODEL
POOL_WINDOWS = (2, 4, 8, 16)
N_POOL_GROUPS = 4
G_C = D_C // N_POOL_GROUPS
POOL_HIST = max(POOL_WINDOWS) - 1
D_FF = int(math.ceil(8 * D_MODEL / 3 / 256)) * 256
ALPHA = (2 * DEPTH) ** 0.25
BETA = (8 * DEPTH) ** -0.25
Q_COLS = H_A * HD
KV_COLS = KVH * HD
GATE_COLS = 3 * H_A
IN_A = Q_COLS + 6 * KV_COLS + GATE_COLS + 2 * C_B
OUT_A = Q_COLS + C_B

kernel_name = 'nsa_conformer_pool_deepnorm_step'


def _layer_norm(x, g, b, eps=1e-5):
    xf = x.astype(jnp.float32)
    mu = jnp.mean(xf, -1, keepdims=True)
    var = jnp.mean(jnp.square(xf - mu), -1, keepdims=True)
    return ((xf - mu) * lax.rsqrt(var + eps) * g + b).astype(x.dtype)


def _rope(x, pos):
    half = HD // 2
    inv = ROPE_THETA ** (-jnp.arange(half, dtype=jnp.float32) / half)
    ang = pos.astype(jnp.float32)[:, None] * inv[None, :]
    cos = jnp.cos(ang)[None, :, None, :]
    sin = jnp.sin(ang)[None, :, None, :]
    xf = x.astype(jnp.float32)
    x1, x2 = xf[..., :half], xf[..., half:]
    return jnp.concatenate([x1 * cos - x2 * sin, x2 * cos + x1 * sin], -1).astype(x.dtype)


def _masked_softmax(s, mask, axes):
    s = jnp.where(mask, s.astype(jnp.float32), NEG_INF)
    m = jnp.max(s, axis=axes, keepdims=True)
    e = jnp.where(mask, jnp.exp(s - m), 0.0)
    return e / jnp.maximum(jnp.sum(e, axis=axes, keepdims=True), TINY)


def _last_rows(a, n):
    if a.shape[1] >= n:
        return a[:, a.shape[1] - n:]
    pad = [(0, 0)] * a.ndim
    pad[1] = (n - a.shape[1], 0)
    return jnp.pad(a, pad)


def _compress(k, w, pe):
    B, L = k.shape[:2]
    nc = L // L_CMP
    kb = k[:, :nc * L_CMP].reshape(B, nc, L_CMP, KVH, HD) + pe[None, None, :, None, :]
    return jnp.einsum('bnpgd,pde->bnge', kb, w)


def _nsa_attend(q, kc, vc, ks, vs, kw, vw, gates, qpos0, kwpos0):
    B, Sq = q.shape[:2]
    L = ks.shape[1]
    nc = kc.shape[1]
    ns = -(-L // L_SEL)
    n_top = min(N_SEL, ns)
    pad = ns * L_SEL - L
    ks_b = jnp.pad(ks, ((0, 0), (0, pad), (0, 0), (0, 0))).reshape(B, ns, L_SEL, KVH, HD).transpose(0, 3, 1, 2, 4)
    vs_b = jnp.pad(vs, ((0, 0), (0, pad), (0, 0), (0, 0))).reshape(B, ns, L_SEL, KVH, HD).transpose(0, 3, 1, 2, 4)
    kw_p = jnp.pad(kw, ((0, 0), (WINDOW, 0), (0, 0), (0, 0)))
    vw_p = jnp.pad(vw, ((0, 0), (WINDOW, 0), (0, 0), (0, 0)))
    qb = Q_BLOCK if Sq % Q_BLOCK == 0 else Sq
    nb = Sq // qb
    span = WINDOW + qb - 1
    q_blk = q.reshape(B, nb, qb, KVH, HPG, HD).transpose(1, 0, 2, 3, 4, 5)
    g_blk = gates.reshape(B, nb, qb, KVH, HPG, 3).transpose(1, 0, 2, 3, 4, 5)
    cmp_end = jnp.arange(nc, dtype=jnp.int32) * L_CMP + (L_CMP - 1)
    blk_ids = jnp.arange(ns, dtype=jnp.int32)
    b_idx = jnp.arange(B)[:, None, None, None]
    g_idx = jnp.arange(KVH)[None, None, :, None]
    scale = HD ** -0.5

    def block(args):
        qi, gt, i = args
        t = qpos0 + i * qb + jnp.arange(qb, dtype=jnp.int32)
        s_c = jnp.einsum('bqgrd,bngd->bqgrn', qi, kc) * scale
        m_c = (cmp_end[None, :] <= t[:, None])[None, :, None, None, :]
        p_c = _masked_softmax(s_c, m_c, (-1,))
        o_c = jnp.einsum('bqgrn,bngd->bqgrd', p_c.astype(vc.dtype), vc)
        imp = jnp.pad(jnp.sum(p_c, axis=3), ((0, 0), (0, 0), (0, 0), (0, ns * (L_SEL // L_CMP) - nc)))
        imp = imp.reshape(B, qb, KVH, ns, L_SEL // L_CMP).sum(-1)
        cur = (t // L_SEL)[:, None]
        forced = (blk_ids[None] == 0) | (blk_ids[None] == cur) | (blk_ids[None] == cur - 1)
        valid = blk_ids[None] * L_SEL <= t[:, None]
        score = jnp.where(valid[None, :, None, :], imp + jnp.where(forced, FORCE_BONUS, 0.0)[None, :, None, :], -FORCE_BONUS)
        _, idx = lax.top_k(score, n_top)
        k_sel = ks_b[b_idx, g_idx, idx]
        v_sel = vs_b[b_idx, g_idx, idx]
        s_s = jnp.einsum('bqgrd,bqgksd->bqgrks', qi, k_sel) * scale
        kpos = idx[..., None] * L_SEL + jnp.arange(L_SEL, dtype=jnp.int32)
        m_s = (kpos <= t[None, :, None, None, None])[:, :, :, None]
        p_s = _masked_softmax(s_s, m_s, (-2, -1))
        o_s = jnp.einsum('bqgrks,bqgksd->bqgrd', p_s.astype(v_sel.dtype), v_sel)
        start = qpos0 + i * qb + 1 - kwpos0
        k_w = lax.dynamic_slice_in_dim(kw_p, start, span, axis=1)
        v_w = lax.dynamic_slice_in_dim(vw_p, start, span, axis=1)
        wpos = (kwpos0 - WINDOW) + start + jnp.arange(span, dtype=jnp.int32)
        m_w = ((wpos[None] >= kwpos0) & (wpos[None] <= t[:, None]) & (wpos[None] > t[:, None] - WINDOW))[None, :, None, None, :]
        s_w = jnp.einsum('bqgrd,bkgd->bqgrk', qi, k_w) * scale
        p_w = _masked_softmax(s_w, m_w, (-1,))
        o_w = jnp.einsum('bqgrk,bkgd->bqgrd', p_w.astype(v_w.dtype), v_w)
        return gt[..., 0:1] * o_c + gt[..., 1:2] * o_s + gt[..., 2:3] * o_w

    o = lax.map(block, (q_blk, g_blk, jnp.arange(nb, dtype=jnp.int32)))
    return o.transpose(1, 0, 2, 3, 4, 5).reshape(B, Sq, H_A * HD)


def _nsa_conv_mixer(x, pos0, wb, past_kv, win_past, conv_past, w_in, w_ck, pe_ck, w_cv, pe_cv,
                    conv_w, conv_b, cln_g, cln_b, w_out):
    B, S, _ = x.shape
    h = x @ w_in
    cuts = [Q_COLS + i * KV_COLS for i in range(7)] + [Q_COLS + 6 * KV_COLS + GATE_COLS,
                                                        Q_COLS + 6 * KV_COLS + GATE_COLS + C_B]
    q, kc, vc, ks, vs, kw, vw, gl, ga, gb = jnp.split(h, cuts, axis=-1)
    pos = pos0 + jnp.arange(S, dtype=jnp.int32)
    q = _rope(q.reshape(B, S, H_A, HD), pos)
    ks = _rope(ks.reshape(B, S, KVH, HD), pos)
    kw = _rope(kw.reshape(B, S, KVH, HD), pos)
    kv_rows = jnp.stack([kc.reshape(B, S, KVH, HD), vc.reshape(B, S, KVH, HD), ks,
                         vs.reshape(B, S, KVH, HD)], axis=2)
    kv_all = kv_rows if past_kv is None else jnp.concatenate([past_kv, kv_rows], axis=1)
    k_cmp = _compress(kv_all[:, :, 0], w_ck, pe_ck)
    v_cmp = _compress(kv_all[:, :, 1], w_cv, pe_cv)
    k_cmp = _rope(k_cmp, jnp.arange(k_cmp.shape[1], dtype=jnp.int32) * L_CMP + (L_CMP - 1))
    win_rows = jnp.stack([kw, vw.reshape(B, S, KVH, HD)], axis=2)
    if win_past is None:
        win_all, kwpos0 = win_rows, pos0
    else:
        win_all, kwpos0 = jnp.concatenate([win_past, win_rows], axis=1), pos0 - win_past.shape[1]
    gates = jax.nn.sigmoid(gl.reshape(B, S, H_A, 3))
    o_a = _nsa_attend(q, k_cmp, v_cmp, kv_all[:, :, 2], kv_all[:, :, 3],
                      win_all[:, :, 0], win_all[:, :, 1], gates, pos0, kwpos0)
    u = ga * jax.nn.sigmoid(gb)
    hist = jnp.zeros((B, CONV_W - 1, C_B), u.dtype) if conv_past is None else conv_past
    ue = jnp.concatenate([hist, u], axis=1)
    c = lax.conv_general_dilated(ue, conv_w[:, None, :].astype(ue.dtype), (1,), 'VALID',
                                 dimension_numbers=('NWC', 'WIO', 'NWC'), feature_group_count=C_B) + conv_b
    c = jax.nn.silu(_layer_norm(c, cln_g, cln_b))
    out = jnp.concatenate([o_a, c], axis=-1) @ w_out
    return out, kv_rows, _last_rows(win_all, wb), ue[:, ue.shape[1] - (CONV_W - 1):]


def _pool_mixer(x, pos0, pool_past, w_in, w_grp, scale, w_out):
    B, S, _ = x.shape
    u = x @ w_in
    hist = jnp.zeros((B, POOL_HIST, D_C), u.dtype) if pool_past is None else pool_past
    ue = jnp.concatenate([hist, u], axis=1)
    cs = jnp.cumsum(ue.astype(jnp.float32), axis=1)
    cs = jnp.concatenate([jnp.zeros((B, 1, D_C), jnp.float32), cs], axis=1)
    pos = pos0 + jnp.arange(S, dtype=jnp.int32)
    upper = cs[:, POOL_HIST + 1:]
    means = []
    for g, w in enumerate(POOL_WINDOWS):
        lo = POOL_HIST + 1 - w
        sl = slice(g * G_C, (g + 1) * G_C)
        cnt = jnp.minimum(pos + 1, w).astype(jnp.float32)[None, :, None]
        means.append((upper[:, :, sl] - cs[:, lo:lo + S, sl]) / cnt)
    d = jnp.concatenate(means, axis=-1) - u.astype(jnp.float32)
    d = d.astype(x.dtype).reshape(B, S, N_POOL_GROUPS, G_C)
    z = jnp.einsum('bsgc,gce->bsge', d, w_grp).reshape(B, S, D_C) * scale
    return z @ w_out, ue[:, ue.shape[1] - POOL_HIST:]


def _swiglu(x, w1, w3, w2):
    return (jax.nn.silu(x @ w1) * (x @ w3)) @ w2


def _trunk(x, pos0, wb, cache_kv, page_table, state_win, state_conv, state_pool, p):
    B = x.shape[0]
    kv_new, win_new, conv_new, pool_new = [], [], [], []
    for l in range(DEPTH):
        if l % 2 == 0:
            ia = l // 2
            if cache_kv is None:
                past_kv, win_past, conv_past = None, None, None
            else:
                past_kv = cache_kv[ia, page_table].reshape(B, -1, 4, KVH, HD)
                win_past, conv_past = state_win[ia], state_conv[ia]
            h, kv_n, win_n, conv_n = _nsa_conv_mixer(
                x, pos0, wb, past_kv, win_past, conv_past,
                p['w_in_a'][ia], p['w_cmp_k'][ia], p['pe_cmp_k'][ia], p['w_cmp_v'][ia], p['pe_cmp_v'][ia],
                p['conv_w'][ia], p['conv_b'][ia], p['conv_ln_g'][ia], p['conv_ln_b'][ia], p['w_out_a'][ia])
            kv_new.append(kv_n)
            win_new.append(win_n)
            conv_new.append(conv_n)
        else:
            ic = l // 2
            pool_past = None if state_pool is None else state_pool[ic]
            h, pool_n = _pool_mixer(x, pos0, pool_past, p['w_in_c'][ic], p['w_grp_c'][ic],
                                    p['scale_c'][ic], p['w_out_c'][ic])
            pool_new.append(pool_n)
        x = _layer_norm(ALPHA * x + h, p['ln1_g'][l], p['ln1_b'][l])
        x = _layer_norm(ALPHA * x + _swiglu(x, p['w1'][l], p['w3'][l], p['w2'][l]), p['ln2_g'][l], p['ln2_b'][l])
    return x, jnp.stack(kv_new), jnp.stack(win_new), jnp.stack(conv_new), jnp.stack(pool_new)


def setup_inputs(seed: int = 0) -> dict:
    key = jax.random.key(seed)
    k = jax.random.split(key, 32)
    n_pages = PAST_LEN // PAGE_SIZE
    n_used = DEC_BATCH * n_pages
    n_pool = n_used + max(1, n_used // 4)
    wb = min(WINDOW, PAST_LEN)

    def nrm(kk, shape, s=1.0):
        return jax.random.normal(kk, shape, jnp.float32) * s

    page_table = jax.random.permutation(k[0], n_pool)[:n_used].reshape(DEC_BATCH, n_pages).astype(jnp.int32)
    return {
        'x_prompt': nrm(k[1], (BATCH, SEQ, D_MODEL)),
        'x_sample': nrm(k[2], (DEC_BATCH, DEC_SEQ, D_MODEL)),
        'cache_kv': nrm(k[3], (N_A, n_pool, PAGE_SIZE, 4, KVH, HD)),
        'state_win': nrm(k[4], (N_A, DEC_BATCH, wb, 2, KVH, HD)),
        'state_conv': nrm(k[5], (N_A, DEC_BATCH, CONV_W - 1, C_B), 0.5),
        'state_pool': nrm(k[6], (N_C, DEC_BATCH, POOL_HIST, D_C)),
        'page_table': page_table,
        'w_in_a': nrm(k[7], (N_A, D_MODEL, IN_A), D_MODEL ** -0.5),
        'w_cmp_k': nrm(k[8], (N_A, L_CMP, HD, HD), (L_CMP * HD) ** -0.5),
        'pe_cmp_k': nrm(k[9], (N_A, L_CMP, HD), 0.1),
        'w_cmp_v': nrm(k[10], (N_A, L_CMP, HD, HD), (L_CMP * HD) ** -0.5),
        'pe_cmp_v': nrm(k[11], (N_A, L_CMP, HD), 0.1),
        'conv_w': nrm(k[12], (N_A, CONV_W, C_B), CONV_W ** -0.5),
        'conv_b': nrm(k[13], (N_A, C_B), 0.01),
        'conv_ln_g': 1.0 + nrm(k[14], (N_A, C_B), 0.02),
        'conv_ln_b': nrm(k[15], (N_A, C_B), 0.01),
        'w_out_a': nrm(k[16], (N_A, OUT_A, D_MODEL), BETA * OUT_A ** -0.5),
        'w_in_c': nrm(k[17], (N_C, D_MODEL, D_C), D_MODEL ** -0.5),
        'w_grp_c': nrm(k[18], (N_C, N_POOL_GROUPS, G_C, G_C), G_C ** -0.5),
        'scale_c': 1.0 + nrm(k[19], (N_C, D_C), 0.02),
        'w_out_c': nrm(k[20], (N_C, D_C, D_MODEL), BETA * D_C ** -0.5),
        'ln1_g': 1.0 + nrm(k[21], (DEPTH, D_MODEL), 0.02),
        'ln1_b': nrm(k[22], (DEPTH, D_MODEL), 0.01),
        'ln2_g': 1.0 + nrm(k[23], (DEPTH, D_MODEL), 0.02),
        'ln2_b': nrm(k[24], (DEPTH, D_MODEL), 0.01),
        'w1': nrm(k[25], (DEPTH, D_MODEL, D_FF), D_MODEL ** -0.5),
        'w3': nrm(k[26], (DEPTH, D_MODEL, D_FF), D_MODEL ** -0.5),
        'w2': nrm(k[27], (DEPTH, D_FF, D_MODEL), BETA * D_FF ** -0.5),
    }


def reference(x_prompt, x_sample, cache_kv, state_win, state_conv, state_pool, page_table,
              w_in_a, w_cmp_k, pe_cmp_k, w_cmp_v, pe_cmp_v, conv_w, conv_b, conv_ln_g, conv_ln_b,
              w_out_a, w_in_c, w_grp_c, scale_c, w_out_c, ln1_g, ln1_b, ln2_g, ln2_b, w1, w3, w2):
    p = dict(w_in_a=w_in_a, w_cmp_k=w_cmp_k, pe_cmp_k=pe_cmp_k, w_cmp_v=w_cmp_v, pe_cmp_v=pe_cmp_v,
             conv_w=conv_w, conv_b=conv_b, conv_ln_g=conv_ln_g, conv_ln_b=conv_ln_b, w_out_a=w_out_a,
             w_in_c=w_in_c, w_grp_c=w_grp_c, scale_c=scale_c, w_out_c=w_out_c,
             ln1_g=ln1_g, ln1_b=ln1_b, ln2_g=ln2_g, ln2_b=ln2_b, w1=w1, w3=w3, w2=w2)
    wb = state_win.shape[2]
    past_len = page_table.shape[1] * cache_kv.shape[2]
    y_prompt, kv_prompt, win_prompt, conv_prompt, pool_prompt = _trunk(
        x_prompt, 0, wb, None, None, None, None, None, p)
    y_sample, kv_sample, win_sample, conv_sample, pool_sample = _trunk(
        x_sample, past_len, wb, cache_kv, page_table, state_win, state_conv, state_pool, p)
    return (y_prompt, y_sample, kv_prompt, kv_sample, win_prompt, win_sample,
            conv_prompt, conv_sample, pool_prompt, pool_sample)
```

```python
import functools

import numpy as np
import jax
import jax.numpy as jnp
from jax import lax
from jax.experimental import pallas as pl
from jax.experimental.pallas import tpu as pltpu

F32 = jnp.float32
BF16 = jnp.bfloat16

D_MODEL = 1024
DEPTH = 4
HD = 64
H_A = 8
KVH = 2
HPG = H_A // KVH
L_CMP = 32
L_SEL = 64
N_SEL = 8
WINDOW = 512
ROPE_THETA = 10000.0
FORCE_BONUS = 1.0e4
NEG_INF = -1.0e30
TINY = 1.0e-30
C_B = D_MODEL // 2
CONV_W = 31
POOL_WINDOWS = (2, 4, 8, 16)
G_C = D_MODEL // len(POOL_WINDOWS)
POOL_HIST = max(POOL_WINDOWS) - 1
D_FF = 2816
ALPHA = (2 * DEPTH) ** 0.25
Q_COLS = H_A * HD
KV_COLS = KVH * HD
GATE_COLS = 3 * H_A
GATE_ROWS = 32
Q_BLOCK = 128
N_CMP = 64
CONV_HALO = 32
POOL_HALO = 16
VMEM_LIMIT = 56 * 1024 * 1024


def _dot(a, b):
    return jnp.dot(a, b, preferred_element_type=F32)


def _dot_nt(a, b):
    return lax.dot_general(a, b, (((1,), (1,)), ((), ())), preferred_element_type=F32)


def _dot_tn(a, b):
    return lax.dot_general(a, b, (((0,), (0,)), ((), ())), preferred_element_type=F32)


def _sigmoid(x):
    return 1.0 / (1.0 + jnp.exp(-x))


def _layer_norm(y, g, b):
    mu = jnp.mean(y, axis=-1, keepdims=True)
    d = y - mu
    var = jnp.mean(d * d, axis=-1, keepdims=True)
    return d * lax.rsqrt(var + 1e-5) * g + b


def _rope128(v, cos, sin_signed):
    lane = lax.broadcasted_iota(jnp.int32, v.shape, 1)
    rot = jnp.where((lane % HD) < HD // 2, pltpu.roll(v, 128 - HD // 2, 1), pltpu.roll(v, HD // 2, 1))
    return v * cos + rot * sin_signed


def _params(sem=None):
    return pltpu.CompilerParams(dimension_semantics=sem, vmem_limit_bytes=VMEM_LIMIT)


def _const_spec(shape):
    nd = len(shape)
    return pl.BlockSpec(shape, lambda *_: (0,) * nd, pipeline_mode=pl.Buffered(1))


def _inproj_a_kernel(x_ref, wq_ref, wkv_ref, wwin_ref, wga_ref, wgb_ref, wglt_ref, cos_ref, sin_ref,
                     q_out, kv_out, win_out, u_out, gt_out):
    xb = x_ref[...].astype(BF16)
    cos = cos_ref[...]
    sin = sin_ref[...]
    q = _dot(xb, wq_ref[...])
    for c in range(Q_COLS // 128):
        q_out[:, c * 128:(c + 1) * 128] = _rope128(q[:, c * 128:(c + 1) * 128], cos, sin) * (HD ** -0.5)
    kv = _dot(xb, wkv_ref[...])
    kv_out[:, 0:256] = kv[:, 0:256]
    kv_out[:, 256:384] = _rope128(kv[:, 256:384], cos, sin)
    kv_out[:, 384:512] = kv[:, 384:512]
    w = _dot(xb, wwin_ref[...])
    win_out[:, 0:128] = _rope128(w[:, 0:128], cos, sin)
    win_out[:, 128:256] = w[:, 128:256]
    u_out[...] = _dot(xb, wga_ref[...]) * _sigmoid(_dot(xb, wgb_ref[...]))
    gt_out[...] = _sigmoid(_dot_nt(wglt_ref[...], xb))


def _inproj_a(x, wts, cos, sin, tm):
    t = x.shape[0]
    n_tab = cos.shape[0] // tm
    row = lambda w: pl.BlockSpec((tm, w), lambda i: (i, 0))
    tab = pl.BlockSpec((tm, 128), lambda i: (i % n_tab, 0))
    return pl.pallas_call(
        _inproj_a_kernel,
        grid=(t // tm,),
        in_specs=[row(D_MODEL)] + [_const_spec(w.shape) for w in wts] + [tab, tab],
        out_specs=[row(Q_COLS), row(4 * KV_COLS), row(2 * KV_COLS), row(C_B),
                   pl.BlockSpec((GATE_ROWS, tm), lambda i: (0, i))],
        out_shape=[jax.ShapeDtypeStruct((t, Q_COLS), F32), jax.ShapeDtypeStruct((t, 4 * KV_COLS), F32),
                   jax.ShapeDtypeStruct((t, 2 * KV_COLS), F32), jax.ShapeDtypeStruct((t, C_B), F32),
                   jax.ShapeDtypeStruct((GATE_ROWS, t), F32)],
        compiler_params=_params(("parallel",)),
        name="inproj_a",
    )(x, *wts, cos, sin)


def _compress(src_ref, pe4_ref, w4_ref, cosc_ref, sinc_ref, kcmp, vcmp):
    acc = jnp.zeros((N_CMP, 4 * HD), F32)
    for p in range(L_CMP):
        xp = jnp.concatenate([src_ref[c, pl.ds(p, N_CMP, stride=L_CMP), :] for c in range(2)], axis=1)
        xp = xp + pe4_ref[p:p + 1, :]
        acc = acc + _dot(xp.astype(BF16), w4_ref[p])
    k = _rope128(acc[:, 0:2 * HD], cosc_ref[...], sinc_ref[...])
    v = acc[:, 2 * HD:4 * HD]
    for g in range(KVH):
        kcmp[g] = k[:, g * HD:(g + 1) * HD].astype(BF16)
        vcmp[g] = v[:, g * HD:(g + 1) * HD].astype(BF16)


def _softmax_step(s, msk, v_blk, carry):
    m, l, acc = carry
    s = jnp.where(msk, s, NEG_INF)
    m_new = jnp.maximum(m, jnp.max(s, axis=0, keepdims=True))
    alpha = jnp.exp(m - m_new)
    p = jnp.where(msk, jnp.exp(s - m_new), 0.0)
    l = alpha * l + jnp.sum(p, axis=0, keepdims=True)
    acc = alpha * acc + _dot_tn(v_blk, p.astype(BF16))
    return m_new, l, acc


def _nsa_core(qf, gt, t_q, kcmp, vcmp, ksel, vsel, kwin, vwin, imp_scr, sel_scr,
              nq, ns_real, ns_pad, sel_lo, sel_hi, win_lo, win_hi, win_pos0):
    w = HPG * nq
    t_lane = jnp.concatenate([t_q] * HPG, axis=1)
    row64 = lax.broadcasted_iota(jnp.int32, (L_SEL, w), 0)
    init = (jnp.full((1, w), NEG_INF, F32), jnp.zeros((1, w), F32), jnp.zeros((HD, w), F32))
    outs = []
    for g in range(KVH):
        qg = jnp.concatenate([qf[:, (g * HPG + h) * HD:(g * HPG + h + 1) * HD] for h in range(HPG)],
                             axis=0).astype(BF16)
        s_c = _dot_nt(kcmp[g], qg)
        m_c = (row64 * L_CMP + (L_CMP - 1)) <= t_lane
        s_c = jnp.where(m_c, s_c, NEG_INF)
        e_c = jnp.where(m_c, jnp.exp(s_c - jnp.max(s_c, axis=0, keepdims=True)), 0.0)
        p_c = e_c / jnp.maximum(jnp.sum(e_c, axis=0, keepdims=True), TINY)
        o_c = _dot_tn(vcmp[g], p_c.astype(BF16))
        imp = p_c[:, 0:nq]
        for h in range(1, HPG):
            imp = imp + p_c[:, h * nq:(h + 1) * nq]
        imp_scr[...] = imp
        half = N_CMP // 2
        imp2 = imp_scr[pl.ds(0, half, stride=2), :] + imp_scr[pl.ds(1, half, stride=2), :]
        if ns_pad > half:
            imp2 = jnp.concatenate([imp2, jnp.zeros((ns_pad - half, nq), F32)], axis=0)
        blk = lax.broadcasted_iota(jnp.int32, (ns_pad, nq), 0)
        cur = t_q // L_SEL
        forced = (blk == 0) | (blk == cur) | (blk == cur - 1)
        valid = (blk * L_SEL) <= t_q
        score = jnp.where(valid, imp2 + jnp.where(forced, FORCE_BONUS, 0.0), -FORCE_BONUS)
        score = jnp.where(blk < ns_real, score, -jnp.inf)
        cnt = jnp.zeros((ns_pad, nq), jnp.int32)
        for k in range(ns_real):
            rk = score[k:k + 1, :]
            beats = (rk > score) | ((rk == score) & (blk > k))
            cnt = cnt + beats.astype(jnp.int32)
        sel = (cnt < N_SEL).astype(F32)
        sel_scr[...] = jnp.concatenate([sel] * HPG, axis=1)

        def sel_body(j, carry):
            off = pl.multiple_of(j * L_SEL, L_SEL)
            s = _dot_nt(ksel[g, pl.ds(off, L_SEL), :], qg)
            kpos = j * L_SEL + row64
            msk = (sel_scr[pl.ds(j, 1), :] > 0.5) & (kpos <= t_lane)
            return _softmax_step(s, msk, vsel[g, pl.ds(off, L_SEL), :], carry)

        _, l_s, a_s = lax.fori_loop(sel_lo, sel_hi, sel_body, init)
        o_s = a_s / jnp.maximum(l_s, TINY)

        def win_body(j, carry):
            off = pl.multiple_of(j * L_SEL, L_SEL)
            s = _dot_nt(kwin[g, pl.ds(off, L_SEL), :], qg)
            kpos = win_pos0 + j * L_SEL + row64
            msk = (kpos <= t_lane) & (kpos > t_lane - WINDOW)
            return _softmax_step(s, msk, vwin[g, pl.ds(off, L_SEL), :], carry)

        _, l_w, a_w = lax.fori_loop(win_lo, win_hi, win_body, init)
        o_w = a_w / jnp.maximum(l_w, TINY)
        for h in range(HPG):
            r = 3 * (g * HPG + h)
            sl = slice(h * nq, (h + 1) * nq)
            outs.append(gt[r:r + 1, :] * o_c[:, sl] + gt[r + 1:r + 2, :] * o_s[:, sl]
                        + gt[r + 2:r + 3, :] * o_w[:, sl])
    return outs


def _attn_prompt_kernel(q_ref, kv_ref, win_ref, gt_ref, pe4_ref, w4_ref, cosc_ref, sinc_ref, o_ref,
                        kcf, ksel, vsel, kwin, vwin, kcmp, vcmp, imp_scr, sel_scr, *, seq):
    i = pl.program_id(1)

    @pl.when(i == 0)
    def _():
        for g in range(KVH):
            ksel[g] = kv_ref[:, 2 * KV_COLS + g * HD:2 * KV_COLS + (g + 1) * HD].astype(BF16)
            vsel[g] = kv_ref[:, 3 * KV_COLS + g * HD:3 * KV_COLS + (g + 1) * HD].astype(BF16)
            kwin[g] = win_ref[:, g * HD:(g + 1) * HD].astype(BF16)
            vwin[g] = win_ref[:, KV_COLS + g * HD:KV_COLS + (g + 1) * HD].astype(BF16)
        for c in range(2):
            kcf[c] = kv_ref[:, c * KV_COLS:(c + 1) * KV_COLS]
        _compress(kcf, pe4_ref, w4_ref, cosc_ref, sinc_ref, kcmp, vcmp)

    t_q = i * Q_BLOCK + lax.broadcasted_iota(jnp.int32, (1, Q_BLOCK), 1)
    blocks_per_q = Q_BLOCK // L_SEL
    hi = (i + 1) * blocks_per_q
    lo_w = jnp.maximum(hi - blocks_per_q - WINDOW // L_SEL, 0)
    outs = _nsa_core(q_ref[...], gt_ref[...], t_q, kcmp, vcmp, ksel, vsel, kwin, vwin, imp_scr, sel_scr,
                     nq=Q_BLOCK, ns_real=seq // L_SEL, ns_pad=seq // L_SEL, sel_lo=0, sel_hi=hi,
                     win_lo=lo_w, win_hi=hi, win_pos0=0)
    for pair in range(H_A // 2):
        o_ref[:, pair * 128:(pair + 1) * 128] = jnp.concatenate(outs[2 * pair:2 * pair + 2], axis=0).T


def _attn_prompt(q, kv, win, gt, pe4, w4, cosc, sinc, batch, seq):
    t = q.shape[0]
    nqb = seq // Q_BLOCK
    qrow = lambda w: pl.BlockSpec((Q_BLOCK, w), lambda b, i: (b * nqb + i, 0))
    srow = lambda w: pl.BlockSpec((seq, w), lambda b, i: (b, 0))
    return pl.pallas_call(
        functools.partial(_attn_prompt_kernel, seq=seq),
        grid=(batch, nqb),
        in_specs=[qrow(Q_COLS), srow(4 * KV_COLS), srow(2 * KV_COLS),
                  pl.BlockSpec((GATE_ROWS, Q_BLOCK), lambda b, i: (0, b * nqb + i)),
                  _const_spec(pe4.shape), _const_spec(w4.shape), _const_spec(cosc.shape), _const_spec(sinc.shape)],
        out_specs=qrow(Q_COLS),
        out_shape=jax.ShapeDtypeStruct((t, Q_COLS), F32),
        scratch_shapes=[pltpu.VMEM((2, seq, KV_COLS), F32)] + [pltpu.VMEM((KVH, seq, HD), BF16)] * 4
        + [pltpu.VMEM((KVH, N_CMP, HD), BF16)] * 2
        + [pltpu.VMEM((N_CMP, Q_BLOCK), F32), pltpu.VMEM((seq // L_SEL, HPG * Q_BLOCK), F32)],
        compiler_params=_params(("parallel", "arbitrary")),
        name="attn_prompt",
    )(q, kv, win, gt, pe4, w4, cosc, sinc)


def _attn_sample_kernel(pt_ref, q_ref, kv_ref, win_ref, gt_ref, sw_ref, *rest, n_pages, page, past):
    del pt_ref
    pages = rest[:n_pages]
    pe4_ref, w4_ref, cosc_ref, sinc_ref, o_ref, wout_ref = rest[n_pages:n_pages + 6]
    kcf, ksel, vsel, kwin, vwin, kcmp, vcmp, imp_scr, sel_scr = rest[n_pages + 6:]
    nq = q_ref.shape[1]
    wb = sw_ref.shape[1]
    for j in range(n_pages):
        rows = slice(j * page, (j + 1) * page)
        for c in range(2):
            kcf[c, rows, :] = pages[j][0, :, c * KV_COLS:(c + 1) * KV_COLS]
        for g in range(KVH):
            ksel[g, rows, :] = pages[j][0, :, 2 * KV_COLS + g * HD:2 * KV_COLS + (g + 1) * HD].astype(BF16)
            vsel[g, rows, :] = pages[j][0, :, 3 * KV_COLS + g * HD:3 * KV_COLS + (g + 1) * HD].astype(BF16)
    kvn = kv_ref[0]
    wn = win_ref[0]
    zpad = jnp.zeros((L_SEL - nq, HD), F32)
    newblk = lambda a: jnp.concatenate([a, zpad], axis=0).astype(BF16)
    for g in range(KVH):
        ksel[g, past:past + L_SEL, :] = newblk(kvn[:, 2 * KV_COLS + g * HD:2 * KV_COLS + (g + 1) * HD])
        vsel[g, past:past + L_SEL, :] = newblk(kvn[:, 3 * KV_COLS + g * HD:3 * KV_COLS + (g + 1) * HD])
        kwin[g, 0:wb, :] = sw_ref[0, :, g * HD:(g + 1) * HD].astype(BF16)
        vwin[g, 0:wb, :] = sw_ref[0, :, KV_COLS + g * HD:KV_COLS + (g + 1) * HD].astype(BF16)
        kwin[g, wb:wb + L_SEL, :] = newblk(wn[:, g * HD:(g + 1) * HD])
        vwin[g, wb:wb + L_SEL, :] = newblk(wn[:, KV_COLS + g * HD:KV_COLS + (g + 1) * HD])
    _compress(kcf, pe4_ref, w4_ref, cosc_ref, sinc_ref, kcmp, vcmp)
    wout_ref[0, 0:wb - nq, :] = sw_ref[0, nq:wb, :]
    wout_ref[0, wb - nq:wb, :] = wn

    t_q = past + lax.broadcasted_iota(jnp.int32, (1, nq), 1)
    ns_real = past // L_SEL + 1
    outs = _nsa_core(q_ref[0], gt_ref[0], t_q, kcmp, vcmp, ksel, vsel, kwin, vwin, imp_scr, sel_scr,
                     nq=nq, ns_real=ns_real, ns_pad=sel_scr.shape[0], sel_lo=0, sel_hi=ns_real,
                     win_lo=0, win_hi=wb // L_SEL + 1, win_pos0=past - wb)
    for hd in range(H_A):
        o_ref[0, hd * HD:(hd + 1) * HD, :] = outs[hd]


def _attn_sample(page_table, q3, kv3, win3, gt3, state_win, cache, pe4, w4, cosc, sinc):
    nb, nq, _ = q3.shape
    n_pages = page_table.shape[1]
    page = cache.shape[1]
    past = n_pages * page
    wb = state_win.shape[1]
    ns_pad = -(-(past // L_SEL + 1) // 8) * 8
    seq3 = lambda r, w: pl.BlockSpec((1, r, w), lambda b, pt: (b, 0, 0))
    page_spec = lambda j: pl.BlockSpec((1, page, 4 * KV_COLS), lambda b, pt: (pt[b, j], 0, 0))
    const = lambda a: pl.BlockSpec(a.shape, lambda b, pt: (0,) * a.ndim, pipeline_mode=pl.Buffered(1))
    grid_spec = pltpu.PrefetchScalarGridSpec(
        num_scalar_prefetch=1,
        grid=(nb,),
        in_specs=[seq3(nq, Q_COLS), seq3(nq, 4 * KV_COLS), seq3(nq, 2 * KV_COLS), seq3(GATE_ROWS, nq),
                  seq3(wb, 2 * KV_COLS)] + [page_spec(j) for j in range(n_pages)]
        + [const(pe4), const(w4), const(cosc), const(sinc)],
        out_specs=[seq3(Q_COLS, nq), seq3(wb, 2 * KV_COLS)],
        scratch_shapes=[pltpu.VMEM((2, past, KV_COLS), F32)]
        + [pltpu.VMEM((KVH, past + L_SEL, HD), BF16)] * 2
        + [pltpu.VMEM((KVH, wb + L_SEL, HD), BF16)] * 2
        + [pltpu.VMEM((KVH, N_CMP, HD), BF16)] * 2
        + [pltpu.VMEM((N_CMP, nq), F32), pltpu.VMEM((ns_pad, HPG * nq), F32)],
    )
    return pl.pallas_call(
        functools.partial(_attn_sample_kernel, n_pages=n_pages, page=page, past=past),
        grid_spec=grid_spec,
        out_shape=[jax.ShapeDtypeStruct((nb, Q_COLS, nq), F32), jax.ShapeDtypeStruct((nb, wb, 2 * KV_COLS), F32)],
        compiler_params=_params(("arbitrary",)),
        name="attn_sample",
    )(page_table, q3, kv3, win3, gt3, state_win, *([cache] * n_pages), pe4, w4, cosc, sinc)


def _conv_kernel(ue_ref, w_ref, b_ref, g_ref, bb_ref, c_ref, *, rows):
    nb, st, ch = c_ref.shape
    pad = CONV_HALO - (CONV_W - 1)
    for r0 in range(0, st, rows):
        acc = jnp.zeros((nb, rows, ch), F32)
        for k in range(CONV_W):
            acc = acc + ue_ref[:, pl.ds(pad + r0 + k, rows), :] * w_ref[k:k + 1, :].reshape(1, 1, ch)
        y = _layer_norm(acc + b_ref[...].reshape(1, 1, ch), g_ref[...].reshape(1, 1, ch), bb_ref[...].reshape(1, 1, ch))
        c_ref[:, r0:r0 + rows, :] = y * _sigmoid(y)


def _conv_module(ue_tiles, w, b, g, bb, nb, st):
    n = ue_tiles.shape[0]
    ch = ue_tiles.shape[2]
    blk = lambda r: pl.BlockSpec((nb, r, ch), lambda i: (i, 0, 0))
    return pl.pallas_call(
        functools.partial(_conv_kernel, rows=min(st, 32)),
        grid=(n // nb,),
        in_specs=[blk(st + CONV_HALO)] + [_const_spec(a.shape) for a in (w, b, g, bb)],
        out_specs=blk(st),
        out_shape=jax.ShapeDtypeStruct((n, st, ch), F32),
        compiler_params=_params(("parallel",)),
        name="conv_module",
    )(ue_tiles, w, b, g, bb)


def _matmul_kernel(x_ref, w_ref, o_ref):
    o_ref[...] = _dot(x_ref[...].astype(BF16), w_ref[...])


def _matmul(x, w, tm):
    t, k = x.shape
    n = w.shape[1]
    return pl.pallas_call(
        _matmul_kernel,
        grid=(t // tm,),
        in_specs=[pl.BlockSpec((tm, k), lambda i: (i, 0)), _const_spec(w.shape)],
        out_specs=pl.BlockSpec((tm, n), lambda i: (i, 0)),
        out_shape=jax.ShapeDtypeStruct((t, n), F32),
        compiler_params=_params(("parallel",)),
        name="pool_inproj",
    )(x, w)


def _pool_kernel(ue_ref, d_ref, *, rows, tiles_per_seq, pos0):
    nb, st, _ = d_ref.shape
    tile = pl.program_id(0) % tiles_per_seq
    for r0 in range(0, st, rows):
        pos = pos0 + tile * st + r0 + lax.broadcasted_iota(jnp.int32, (1, rows, G_C), 1)
        for g, wdw in enumerate(POOL_WINDOWS):
            cols = slice(g * G_C, (g + 1) * G_C)
            acc = ue_ref[:, pl.ds(POOL_HALO + r0, rows), cols]
            u = acc
            for k in range(1, wdw):
                acc = acc + ue_ref[:, pl.ds(POOL_HALO + r0 - k, rows), cols]
            cnt = jnp.minimum(pos + 1, wdw).astype(F32)
            d_ref[:, r0:r0 + rows, cols] = acc / cnt - u


def _pool_diff(ue_tiles, nb, st, tiles_per_seq, pos0):
    n = ue_tiles.shape[0]
    ch = ue_tiles.shape[2]
    blk = lambda r: pl.BlockSpec((nb, r, ch), lambda i: (i, 0, 0))
    return pl.pallas_call(
        functools.partial(_pool_kernel, rows=min(st, 64), tiles_per_seq=tiles_per_seq, pos0=pos0),
        grid=(n // nb,),
        in_specs=[blk(st + POOL_HALO)],
        out_specs=blk(st),
        out_shape=jax.ShapeDtypeStruct((n, st, ch), F32),
        compiler_params=_params(("parallel",)),
        name="pool_diff",
    )(ue_tiles)


def _ffn_ln(x1, w1_ref, w3_ref, w2_ref, g2_ref, b2_ref):
    xb = x1.astype(BF16)
    h1 = _dot(xb, w1_ref[...])
    h3 = _dot(xb, w3_ref[...])
    gated = (h1 * _sigmoid(h1) * h3).astype(BF16)
    return _layer_norm(ALPHA * x1 + _dot(gated, w2_ref[...]), g2_ref[...], b2_ref[...])


def _post_a_kernel(x_ref, oa_ref, c_ref, wo1_ref, wo2_ref, g1_ref, b1_ref, w1_ref, w3_ref, w2_ref, g2_ref, b2_ref,
                   o_ref):
    y = _dot(oa_ref[...].astype(BF16), wo1_ref[...]) + _dot(c_ref[...].astype(BF16), wo2_ref[...])
    x1 = _layer_norm(ALPHA * x_ref[...] + y, g1_ref[...], b1_ref[...])
    o_ref[...] = _ffn_ln(x1, w1_ref, w3_ref, w2_ref, g2_ref, b2_ref)


def _post_c_kernel(x_ref, d_ref, wg_ref, sc_ref, wo_ref, g1_ref, b1_ref, w1_ref, w3_ref, w2_ref, g2_ref, b2_ref,
                   o_ref):
    db = d_ref[...].astype(BF16)
    z = jnp.concatenate([_dot(db[:, g * G_C:(g + 1) * G_C], wg_ref[g]) for g in range(len(POOL_WINDOWS))], axis=1)
    y = _dot((z * sc_ref[...]).astype(BF16), wo_ref[...])
    x1 = _layer_norm(ALPHA * x_ref[...] + y, g1_ref[...], b1_ref[...])
    o_ref[...] = _ffn_ln(x1, w1_ref, w3_ref, w2_ref, g2_ref, b2_ref)


def _post(body, x, acts, consts, tm):
    t = x.shape[0]
    row = lambda a: pl.BlockSpec((tm, a.shape[1]), lambda i: (i, 0))
    return pl.pallas_call(
        body,
        grid=(t // tm,),
        in_specs=[row(x)] + [row(a) for a in acts] + [_const_spec(c.shape) for c in consts],
        out_specs=row(x),
        out_shape=jax.ShapeDtypeStruct(x.shape, F32),
        compiler_params=_params(("parallel",)),
        name=body.__name__.strip("_"),
    )(x, *acts, *consts)


def _rope_tables(pos):
    half = HD // 2
    inv = ROPE_THETA ** (-jnp.arange(half, dtype=F32) / half)
    ang = pos.astype(F32)[:, None] * inv[None, :]
    cos, sin = jnp.cos(ang), jnp.sin(ang)
    return jnp.tile(cos, (1, 4)), jnp.tile(jnp.concatenate([-sin, sin], axis=1), (1, 2))


def _halo_tiles(u3, hist, halo, ts):
    b, s, c = u3.shape
    if hist is None:
        ue = jnp.pad(u3, ((0, 0), (halo, 0), (0, 0)))
    else:
        ue = jnp.concatenate([jnp.zeros((b, halo - hist.shape[1], c), u3.dtype), hist, u3], axis=1)
    nt = s // ts
    if nt == 1:
        return ue
    return jnp.stack([ue[:, i * ts:i * ts + ts + halo] for i in range(nt)], axis=1).reshape(b * nt, ts + halo, c)


def _prep_weights(p):
    n_a = p["w_in_a"].shape[0]
    cuts = np.cumsum([0, Q_COLS, 4 * KV_COLS, 2 * KV_COLS, GATE_COLS, C_B, C_B])
    out = {"a": [], "c": [], "ffn": []}
    for ia in range(n_a):
        w_in = p["w_in_a"][ia]
        wq, wkv, wwin, wgl, wga, wgb = [w_in[:, cuts[k]:cuts[k + 1]].astype(BF16) for k in range(6)]
        wglt = jnp.pad(wgl.T, ((0, GATE_ROWS - GATE_COLS), (0, 0)))
        eye = jnp.eye(4, dtype=F32)
        wk, wv = p["w_cmp_k"][ia], p["w_cmp_v"][ia]
        blocks = jnp.stack([wk, wk, wv, wv], axis=1)
        w4 = jnp.einsum("pkde,kl->pkdle", blocks, eye).reshape(L_CMP, 4 * HD, 4 * HD).astype(BF16)
        pe4 = jnp.concatenate([p["pe_cmp_k"][ia]] * 2 + [p["pe_cmp_v"][ia]] * 2, axis=1)
        wo = p["w_out_a"][ia].astype(BF16)
        out["a"].append(dict(
            inproj=(wq, wkv, wwin, wga, wgb, wglt), w4=w4, pe4=pe4,
            conv=(jnp.pad(p["conv_w"][ia], ((0, 1), (0, 0))), p["conv_b"][ia][None], p["conv_ln_g"][ia][None],
                  p["conv_ln_b"][ia][None]),
            wo1=wo[:Q_COLS], wo2=wo[Q_COLS:]))
    for ic in range(p["w_in_c"].shape[0]):
        out["c"].append(dict(w_in=p["w_in_c"][ic].astype(BF16), w_grp=p["w_grp_c"][ic].astype(BF16),
                             scale=p["scale_c"][ic][None], w_out=p["w_out_c"][ic].astype(BF16)))
    for l in range(DEPTH):
        out["ffn"].append((p["ln1_g"][l][None], p["ln1_b"][l][None], p["w1"][l].astype(BF16),
                           p["w3"][l].astype(BF16), p["w2"][l].astype(BF16), p["ln2_g"][l][None], p["ln2_b"][l][None]))
    return out


def _trunk(x3, pos0, wb, wts, cache_kv, page_table, state_win, state_conv, state_pool):
    b, s, _ = x3.shape
    t = b * s
    prompt = cache_kv is None
    tm = min(256, t)
    x = x3.reshape(t, D_MODEL)
    cos, sin = _rope_tables(pos0 + jnp.arange(s, dtype=jnp.int32))
    if not prompt:
        cos, sin = jnp.tile(cos, (tm // s, 1)), jnp.tile(sin, (tm // s, 1))
    cosc, sinc = _rope_tables(jnp.arange(N_CMP, dtype=jnp.int32) * L_CMP + (L_CMP - 1))
    ts = 512 if prompt else s
    nb = 1 if prompt else 8
    kv_new, win_new, conv_new, pool_new = [], [], [], []
    for l in range(DEPTH):
        ln1_g, ln1_b, w1, w3, w2, ln2_g, ln2_b = wts["ffn"][l]
        if l % 2 == 0:
            a = wts["a"][l // 2]
            q, kv, win, u, gt = _inproj_a(x, a["inproj"], cos, sin, tm)
            kv_new.append(kv.reshape(b, s, 4, KVH, HD))
            if prompt:
                oa = _attn_prompt(q, kv, win, gt, a["pe4"], a["w4"], cosc, sinc, b, s)
                win_new.append(win.reshape(b, s, 2, KVH, HD)[:, s - wb:])
                hist = None
            else:
                ia = l // 2
                gt3 = gt.reshape(GATE_ROWS, b, s).transpose(1, 0, 2)
                sw = state_win[ia].reshape(b, -1, 2 * KV_COLS)
                cache = cache_kv[ia].reshape(cache_kv.shape[1], cache_kv.shape[2], 4 * KV_COLS)
                oat, wout = _attn_sample(page_table, q.reshape(b, s, Q_COLS), kv.reshape(b, s, 4 * KV_COLS),
                                         win.reshape(b, s, 2 * KV_COLS), gt3, sw, cache,
                                         a["pe4"], a["w4"], cosc, sinc)
                oa = oat.transpose(0, 2, 1).reshape(t, Q_COLS)
                win_new.append(wout.reshape(b, -1, 2, KVH, HD))
                hist = state_conv[ia]
            u3 = u.reshape(b, s, C_B)
            ue = _halo_tiles(u3, hist, CONV_HALO, ts)
            c = _conv_module(ue, *a["conv"], nb, ts).reshape(t, C_B)
            keep = CONV_W - 1
            conv_new.append(u3[:, s - keep:] if hist is None else jnp.concatenate([hist, u3], axis=1)[:, -keep:])
            x = _post(_post_a_kernel, x, (oa, c), (a["wo1"], a["wo2"], ln1_g, ln1_b, w1, w3, w2, ln2_g, ln2_b), tm)
        else:
            cw = wts["c"][l // 2]
            u3 = _matmul(x, cw["w_in"], tm).reshape(b, s, D_MODEL)
            hist = None if prompt else state_pool[l // 2]
            ue = _halo_tiles(u3, hist, POOL_HALO, ts)
            d = _pool_diff(ue, nb, ts, s // ts, pos0).reshape(t, D_MODEL)
            pool_new.append(u3[:, s - POOL_HIST:] if hist is None
                            else jnp.concatenate([hist, u3], axis=1)[:, -POOL_HIST:])
            x = _post(_post_c_kernel, x, (d,),
                      (cw["w_grp"], cw["scale"], cw["w_out"], ln1_g, ln1_b, w1, w3, w2, ln2_g, ln2_b), tm)
    return x.reshape(b, s, D_MODEL), jnp.stack(kv_new), jnp.stack(win_new), jnp.stack(conv_new), jnp.stack(pool_new)


def kernel(x_prompt, x_sample, cache_kv, state_win, state_conv, state_pool, page_table, w_in_a, w_cmp_k, pe_cmp_k, w_cmp_v, pe_cmp_v, conv_w, conv_b, conv_ln_g, conv_ln_b, w_out_a, w_in_c, w_grp_c, scale_c, w_out_c, ln1_g, ln1_b, ln2_g, ln2_b, w1, w3, w2):
    p = dict(w_in_a=w_in_a, w_cmp_k=w_cmp_k, pe_cmp_k=pe_cmp_k, w_cmp_v=w_cmp_v, pe_cmp_v=pe_cmp_v,
             conv_w=conv_w, conv_b=conv_b, conv_ln_g=conv_ln_g, conv_ln_b=conv_ln_b, w_out_a=w_out_a,
             w_in_c=w_in_c, w_grp_c=w_grp_c, scale_c=scale_c, w_out_c=w_out_c,
             ln1_g=ln1_g, ln1_b=ln1_b, ln2_g=ln2_g, ln2_b=ln2_b, w1=w1, w3=w3, w2=w2)
    wts = _prep_weights(p)
    past_len = page_table.shape[1] * cache_kv.shape[2]
    wb = state_win.shape[2]
    y_p, kv_p, win_p, conv_p, pool_p = _trunk(x_prompt, 0, wb, wts, None, None, None, None, None)
    y_s, kv_s, win_s, conv_s, pool_s = _trunk(x_sample, past_len, wb, wts, cache_kv, page_table, state_win,
                                              state_conv, state_pool)
    return (y_p, y_s, kv_p, kv_s, win_p, win_s, conv_p, conv_s, pool_p, pool_s)
```

```python
import functools

import numpy as np
import jax
import jax.numpy as jnp
from jax import lax
from jax.experimental import pallas as pl
from jax.experimental.pallas import tpu as pltpu

F32 = jnp.float32
BF16 = jnp.bfloat16

D_MODEL = 1024
DEPTH = 4
HD = 64
H_A = 8
KVH = 2
HPG = H_A // KVH
L_CMP = 32
L_SEL = 64
N_SEL = 8
WINDOW = 512
ROPE_THETA = 10000.0
FORCE_BONUS = 1.0e4
NEG_INF = -1.0e30
TINY = 1.0e-30
C_B = D_MODEL // 2
CONV_W = 31
POOL_WINDOWS = (2, 4, 8, 16)
G_C = D_MODEL // len(POOL_WINDOWS)
POOL_HIST = max(POOL_WINDOWS) - 1
D_FF = 2816
ALPHA = (2 * DEPTH) ** 0.25
Q_COLS = H_A * HD
KV_COLS = KVH * HD
GATE_COLS = 3 * H_A
GATE_ROWS = 32
Q_BLOCK = 128
N_CMP = 64
SEL_CHUNK = 256
WIN_SPAN = WINDOW + Q_BLOCK
CONV_HALO = 32
POOL_HALO = 16
VMEM_LIMIT = 56 * 1024 * 1024


def _dot(a, b):
    return jnp.dot(a, b, preferred_element_type=F32)


def _dot_nt(a, b):
    return lax.dot_general(a, b, (((1,), (1,)), ((), ())), preferred_element_type=F32)


def _dot_tn(a, b):
    return lax.dot_general(a, b, (((0,), (0,)), ((), ())), preferred_element_type=F32)


def _sigmoid(x):
    return 1.0 / (1.0 + jnp.exp(-x))


def _layer_norm(y, g, b):
    mu = jnp.mean(y, axis=-1, keepdims=True)
    d = y - mu
    var = jnp.mean(d * d, axis=-1, keepdims=True)
    return d * lax.rsqrt(var + 1e-5) * g + b


def _rope128(v, cos, sin_signed):
    lane = lax.broadcasted_iota(jnp.int32, v.shape, 1)
    rot = jnp.where((lane % HD) < HD // 2, pltpu.roll(v, 128 - HD // 2, 1), pltpu.roll(v, HD // 2, 1))
    return v * cos + rot * sin_signed


def _params(sem=None):
    return pltpu.CompilerParams(dimension_semantics=sem, vmem_limit_bytes=VMEM_LIMIT)


def _const_spec(shape):
    nd = len(shape)
    return pl.BlockSpec(shape, lambda *_: (0,) * nd, pipeline_mode=pl.Buffered(1))


def _inproj_a_kernel(x_ref, wq_ref, wkv_ref, wwin_ref, wga_ref, wgb_ref, wgl_ref, cos_ref, sin_ref,
                     q_out, kv_out, win_out, u_out, g_out, *, gates_transposed):
    xb = x_ref[...].astype(BF16)
    cos = cos_ref[...]
    sin = sin_ref[...]
    q = _dot(xb, wq_ref[...])
    for c in range(Q_COLS // 128):
        q_out[:, c * 128:(c + 1) * 128] = _rope128(q[:, c * 128:(c + 1) * 128], cos, sin) * (HD ** -0.5)
    kv = _dot(xb, wkv_ref[...])
    kv_out[:, 0:256] = kv[:, 0:256]
    kv_out[:, 256:384] = _rope128(kv[:, 256:384], cos, sin)
    kv_out[:, 384:512] = kv[:, 384:512]
    w = _dot(xb, wwin_ref[...])
    win_out[:, 0:128] = _rope128(w[:, 0:128], cos, sin)
    win_out[:, 128:256] = w[:, 128:256]
    u_out[...] = _dot(xb, wga_ref[...]) * _sigmoid(_dot(xb, wgb_ref[...]))
    g_out[...] = _sigmoid(_dot_nt(wgl_ref[...], xb) if gates_transposed else _dot(xb, wgl_ref[...]))


def _inproj_a(x, wts, cos, sin, tm, gates_transposed):
    t = x.shape[0]
    g_spec = pl.BlockSpec((GATE_ROWS, tm), lambda i: (0, i)) if gates_transposed else pl.BlockSpec((tm, 128), lambda i: (i, 0))
    g_shape = (GATE_ROWS, t) if gates_transposed else (t, 128)
    n_tab = cos.shape[0] // tm
    row = lambda w: pl.BlockSpec((tm, w), lambda i: (i, 0))
    tab = pl.BlockSpec((tm, 128), lambda i: (i % n_tab, 0))
    return pl.pallas_call(
        functools.partial(_inproj_a_kernel, gates_transposed=gates_transposed),
        grid=(t // tm,),
        in_specs=[row(D_MODEL)] + [_const_spec(w.shape) for w in wts] + [tab, tab],
        out_specs=[row(Q_COLS), row(4 * KV_COLS), row(2 * KV_COLS), row(C_B), g_spec],
        out_shape=[jax.ShapeDtypeStruct((t, Q_COLS), F32), jax.ShapeDtypeStruct((t, 4 * KV_COLS), F32),
                   jax.ShapeDtypeStruct((t, 2 * KV_COLS), F32), jax.ShapeDtypeStruct((t, C_B), F32),
                   jax.ShapeDtypeStruct(g_shape, F32)],
        compiler_params=_params(("parallel",)),
        name="inproj_a",
    )(x, *wts, cos, sin)


def _compress(src_ref, pe4_ref, w4_ref, cosc_ref, sinc_ref, kcmp, vcmp):
    acc = jnp.zeros((N_CMP, 4 * HD), F32)
    for p in range(L_CMP):
        xp = jnp.concatenate([src_ref[c, pl.ds(p, N_CMP, stride=L_CMP), :] for c in range(2)], axis=1)
        xp = xp + pe4_ref[p:p + 1, :]
        acc = acc + _dot(xp.astype(BF16), w4_ref[p])
    k = _rope128(acc[:, 0:2 * HD], cosc_ref[...], sinc_ref[...])
    v = acc[:, 2 * HD:4 * HD]
    for g in range(KVH):
        kcmp[g] = k[:, g * HD:(g + 1) * HD].astype(BF16)
        vcmp[g] = v[:, g * HD:(g + 1) * HD].astype(BF16)


def _softmax_t(s, msk):
    s = jnp.where(msk, s, NEG_INF)
    m = jnp.max(s, axis=0, keepdims=True)
    p = jnp.where(msk, jnp.exp(s - m), 0.0)
    return m, p, jnp.sum(p, axis=0, keepdims=True)


def _nsa_core_t(qf, gt, t_q, t0, kcmp, vcmp, ksel, vsel, kwin, vwin, imp_scr, sel_scr, n_sel_blocks):
    nq = Q_BLOCK
    w = HPG * nq
    t_lane = jnp.concatenate([t_q] * HPG, axis=1)
    row_c = lax.broadcasted_iota(jnp.int32, (N_CMP, w), 0)
    row_s = lax.broadcasted_iota(jnp.int32, (SEL_CHUNK, w), 0)
    row_w = lax.broadcasted_iota(jnp.int32, (WIN_SPAN, w), 0)
    sel_per_chunk = SEL_CHUNK // L_SEL
    n_chunks = (t0 + nq + SEL_CHUNK - 1) // SEL_CHUNK
    win_lo = pl.multiple_of(jnp.maximum(t0 + nq - WIN_SPAN, 0), nq)
    outs = []
    for g in range(KVH):
        qg = jnp.concatenate([qf[:, (g * HPG + h) * HD:(g * HPG + h + 1) * HD] for h in range(HPG)],
                             axis=0).astype(BF16)
        m_c = (row_c * L_CMP + (L_CMP - 1)) <= t_lane
        _, e_c, l_c = _softmax_t(_dot_nt(kcmp[g], qg), m_c)
        p_c = e_c / jnp.maximum(l_c, TINY)
        o_c = _dot_tn(vcmp[g], p_c.astype(BF16))
        imp = p_c[:, 0:nq]
        for h in range(1, HPG):
            imp = imp + p_c[:, h * nq:(h + 1) * nq]
        imp_scr[...] = imp
        half = N_CMP // 2
        imp2 = imp_scr[pl.ds(0, half, stride=2), :] + imp_scr[pl.ds(1, half, stride=2), :]
        blk = lax.broadcasted_iota(jnp.int32, (n_sel_blocks, nq), 0)
        cur = t_q // L_SEL
        forced = (blk == 0) | (blk == cur) | (blk == cur - 1)
        valid = (blk * L_SEL) <= t_q
        score = jnp.where(valid, imp2 + jnp.where(forced, FORCE_BONUS, 0.0), -FORCE_BONUS)
        cnt = jnp.zeros((n_sel_blocks, nq), jnp.int32)
        for k in range(n_sel_blocks):
            rk = score[k:k + 1, :]
            beats = (rk > score) | ((rk == score) & (blk > k))
            cnt = cnt + beats.astype(jnp.int32)
        sel = (cnt < N_SEL).astype(F32)
        sel_scr[...] = jnp.concatenate([sel] * HPG, axis=1)

        def sel_body(j, carry):
            m, l, acc = carry
            off = pl.multiple_of(j * SEL_CHUNK, SEL_CHUNK)
            s = _dot_nt(ksel[g, pl.ds(off, SEL_CHUNK), :], qg)
            picked = jnp.concatenate(
                [jnp.broadcast_to(sel_scr[pl.ds(j * sel_per_chunk + r, 1), :], (L_SEL, w))
                 for r in range(sel_per_chunk)], axis=0)
            msk = (picked > 0.5) & ((off + row_s) <= t_lane)
            m_j, p, l_j = _softmax_t(s, msk)
            m_new = jnp.maximum(m, m_j)
            a_old = jnp.exp(m - m_new)
            a_new = jnp.exp(m_j - m_new)
            pv = _dot_tn(vsel[g, pl.ds(off, SEL_CHUNK), :], p.astype(BF16))
            return m_new, a_old * l + a_new * l_j, a_old * acc + a_new * pv

        init = (jnp.full((1, w), NEG_INF, F32), jnp.zeros((1, w), F32), jnp.zeros((HD, w), F32))
        _, l_s, a_s = lax.fori_loop(0, n_chunks, sel_body, init)
        o_s = a_s / jnp.maximum(l_s, TINY)
        kpos = win_lo + row_w
        m_w = (kpos <= t_lane) & (kpos > t_lane - WINDOW)
        _, p_w, l_w = _softmax_t(_dot_nt(kwin[g, pl.ds(win_lo, WIN_SPAN), :], qg), m_w)
        o_w = _dot_tn(vwin[g, pl.ds(win_lo, WIN_SPAN), :], p_w.astype(BF16)) / jnp.maximum(l_w, TINY)
        for h in range(HPG):
            r = 3 * (g * HPG + h)
            sl = slice(h * nq, (h + 1) * nq)
            outs.append(gt[r:r + 1, :] * o_c[:, sl] + gt[r + 1:r + 2, :] * o_s[:, sl]
                        + gt[r + 2:r + 3, :] * o_w[:, sl])
    return outs


def _attn_prompt_kernel(q_ref, kv_ref, win_ref, gt_ref, pe4_ref, w4_ref, cosc_ref, sinc_ref, o_ref,
                        kcf, ksel, vsel, kwin, vwin, kcmp, vcmp, imp_scr, sel_scr, *, seq):
    i = pl.program_id(1)

    @pl.when(i == 0)
    def _():
        for g in range(KVH):
            ksel[g] = kv_ref[:, 2 * KV_COLS + g * HD:2 * KV_COLS + (g + 1) * HD].astype(BF16)
            vsel[g] = kv_ref[:, 3 * KV_COLS + g * HD:3 * KV_COLS + (g + 1) * HD].astype(BF16)
            kwin[g] = win_ref[:, g * HD:(g + 1) * HD].astype(BF16)
            vwin[g] = win_ref[:, KV_COLS + g * HD:KV_COLS + (g + 1) * HD].astype(BF16)
        for c in range(2):
            kcf[c] = kv_ref[:, c * KV_COLS:(c + 1) * KV_COLS]
        _compress(kcf, pe4_ref, w4_ref, cosc_ref, sinc_ref, kcmp, vcmp)

    t0 = i * Q_BLOCK
    t_q = t0 + lax.broadcasted_iota(jnp.int32, (1, Q_BLOCK), 1)
    outs = _nsa_core_t(q_ref[...], gt_ref[...], t_q, t0, kcmp, vcmp, ksel, vsel, kwin, vwin, imp_scr, sel_scr,
                       n_sel_blocks=seq // L_SEL)
    for pair in range(H_A // 2):
        o_ref[:, pair * 128:(pair + 1) * 128] = jnp.concatenate(outs[2 * pair:2 * pair + 2], axis=0).T


def _attn_prompt(q, kv, win, gt, pe4, w4, cosc, sinc, batch, seq):
    t = q.shape[0]
    nqb = seq // Q_BLOCK
    qrow = lambda w: pl.BlockSpec((Q_BLOCK, w), lambda b, i: (b * nqb + i, 0))
    srow = lambda w: pl.BlockSpec((seq, w), lambda b, i: (b, 0))
    return pl.pallas_call(
        functools.partial(_attn_prompt_kernel, seq=seq),
        grid=(batch, nqb),
        in_specs=[qrow(Q_COLS), srow(4 * KV_COLS), srow(2 * KV_COLS),
                  pl.BlockSpec((GATE_ROWS, Q_BLOCK), lambda b, i: (0, b * nqb + i)),
                  _const_spec(pe4.shape), _const_spec(w4.shape), _const_spec(cosc.shape), _const_spec(sinc.shape)],
        out_specs=qrow(Q_COLS),
        out_shape=jax.ShapeDtypeStruct((t, Q_COLS), F32),
        scratch_shapes=[pltpu.VMEM((2, seq, KV_COLS), F32)] + [pltpu.VMEM((KVH, seq, HD), BF16)] * 4
        + [pltpu.VMEM((KVH, N_CMP, HD), BF16)] * 2
        + [pltpu.VMEM((N_CMP, Q_BLOCK), F32), pltpu.VMEM((seq // L_SEL, HPG * Q_BLOCK), F32)],
        compiler_params=_params(("parallel", "arbitrary")),
        name="attn_prompt",
    )(q, kv, win, gt, pe4, w4, cosc, sinc)


def _softmax_rows(parts):
    masked = [jnp.where(k, s, NEG_INF) for s, k in parts]
    m = masked[0].max(axis=1, keepdims=True)
    for s in masked[1:]:
        m = jnp.maximum(m, s.max(axis=1, keepdims=True))
    es = [jnp.where(k, jnp.exp(s - m), 0.0) for s, (_, k) in zip(masked, parts)]
    l = es[0].sum(axis=1, keepdims=True)
    for e in es[1:]:
        l = l + e.sum(axis=1, keepdims=True)
    return es, jnp.maximum(l, TINY)


def _attn_sample_kernel(pt_ref, q_ref, kv_ref, win_ref, gate_ref, sw_ref, *rest, n_pages, page, past):
    del pt_ref
    pages = rest[:n_pages]
    pe4_ref, w4_ref, cosc_ref, sinc_ref, expand_ref, o_ref, wout_ref = rest[n_pages:n_pages + 7]
    kcf, kt, vt, kcmp, vcmp = rest[n_pages + 7:]
    nq = q_ref.shape[1]
    wb = sw_ref.shape[2]
    rows = HPG * nq
    new_pad = 16
    for j in range(n_pages):
        span = slice(j * page, (j + 1) * page)
        for c in range(2):
            kcf[c, span, :] = pages[j][0, c * KV_COLS:(c + 1) * KV_COLS, :].T
        for g in range(KVH):
            kt[g, :, span] = pages[j][0, 2 * KV_COLS + g * HD:2 * KV_COLS + (g + 1) * HD, :].astype(BF16)
            vt[g, :, span] = pages[j][0, 3 * KV_COLS + g * HD:3 * KV_COLS + (g + 1) * HD, :].astype(BF16)
    _compress(kcf, pe4_ref, w4_ref, cosc_ref, sinc_ref, kcmp, vcmp)

    qf = q_ref[0]
    kvn = kv_ref[0]
    wn = win_ref[0]
    gates = gate_ref[0]
    wout_ref[0, :, 0:wb - nq] = sw_ref[0, :, nq:wb]
    wout_ref[0, :, wb - nq:wb] = wn.T

    zpad = jnp.zeros((new_pad - nq, HD), F32)
    new_rows = lambda a: jnp.concatenate([a, zpad], axis=0).astype(BF16)
    ns_real = past // L_SEL + 1
    lane = lax.broadcasted_iota(jnp.int32, (nq, 128), 1)
    blk = lane // 2
    t_sel = past + lax.broadcasted_iota(jnp.int32, (nq, 128), 0)
    iq = lambda n: lax.broadcasted_iota(jnp.int32, (rows, n), 0) % nq
    ik = lambda n: lax.broadcasted_iota(jnp.int32, (rows, n), 1)
    causal_new = (ik(new_pad) <= iq(new_pad)) & (ik(new_pad) < nq)
    for g in range(KVH):
        qg = jnp.concatenate([qf[:, (g * HPG + h) * HD:(g * HPG + h + 1) * HD] for h in range(HPG)],
                             axis=0).astype(BF16)
        m_c = (ik(N_CMP) * L_CMP + (L_CMP - 1)) <= past + iq(N_CMP)
        (e_c,), l_c = _softmax_rows([(_dot_nt(qg, kcmp[g]), m_c)])
        p_c = e_c / l_c
        o_c = _dot(p_c.astype(BF16), vcmp[g])
        imp = p_c[0:nq]
        for h in range(1, HPG):
            imp = imp + p_c[h * nq:(h + 1) * nq]
        imp = jnp.concatenate([imp, jnp.zeros((nq, 128 - N_CMP), F32)], axis=1)
        imp2 = imp + pltpu.roll(imp, 127, 1)
        cur = t_sel // L_SEL
        forced = (blk == 0) | (blk == cur) | (blk == cur - 1)
        valid = (blk * L_SEL) <= t_sel
        score = jnp.where(valid, imp2 + jnp.where(forced, FORCE_BONUS, 0.0), -FORCE_BONUS)
        score = jnp.where(((lane & 1) == 0) & (blk < ns_real), score, -jnp.inf)
        cnt = jnp.zeros((nq, 128), jnp.int32)
        for k in range(ns_real):
            ck = score[:, 2 * k:2 * k + 1]
            beats = (ck > score) | ((ck == score) & (lane > 2 * k))
            cnt = cnt + beats.astype(jnp.int32)
        sel = (cnt < N_SEL).astype(BF16)
        picked = _dot(jnp.concatenate([sel] * HPG, axis=0), expand_ref[...]) > 0.5
        s_p = _dot(qg, kt[g])
        s_n = _dot_nt(qg, new_rows(kvn[:, 2 * KV_COLS + g * HD:2 * KV_COLS + (g + 1) * HD]))
        (e_p, e_n), l_s = _softmax_rows([(s_p, picked[:, 0:past]),
                                         (s_n, picked[:, past:past + new_pad] & causal_new)])
        o_s = (_dot_nt(e_p.astype(BF16), vt[g])
               + _dot(e_n.astype(BF16), new_rows(kvn[:, 3 * KV_COLS + g * HD:3 * KV_COLS + (g + 1) * HD]))) / l_s
        s_wp = _dot(qg, sw_ref[0, g * HD:(g + 1) * HD, :].astype(BF16))
        s_wn = _dot_nt(qg, new_rows(wn[:, g * HD:(g + 1) * HD]))
        (e_wp, e_wn), l_w = _softmax_rows([(s_wp, ik(wb) > iq(wb) + (wb - WINDOW)), (s_wn, causal_new)])
        o_w = (_dot_nt(e_wp.astype(BF16), sw_ref[0, KV_COLS + g * HD:KV_COLS + (g + 1) * HD, :].astype(BF16))
               + _dot(e_wn.astype(BF16), new_rows(wn[:, KV_COLS + g * HD:KV_COLS + (g + 1) * HD]))) / l_w
        gate = lambda br: jnp.concatenate(
            [gates[:, 3 * (g * HPG + h) + br:3 * (g * HPG + h) + br + 1] for h in range(HPG)], axis=0)
        o = gate(0) * o_c + gate(1) * o_s + gate(2) * o_w
        for pair in range(HPG // 2):
            o_ref[0, :, (g * HPG + 2 * pair) * HD:(g * HPG + 2 * pair + 2) * HD] = jnp.concatenate(
                [o[2 * pair * nq:(2 * pair + 1) * nq], o[(2 * pair + 1) * nq:(2 * pair + 2) * nq]], axis=1)


def _attn_sample(page_ids, win_off, q3, kv3, win3, gates3, state_win_t, cache_t, pe4, w4, cosc, sinc, expand):
    nb, nq, _ = q3.shape
    n_pages = page_ids.shape[1]
    page = cache_t.shape[2]
    past = n_pages * page
    wb = state_win_t.shape[2]
    seq3 = lambda r, w: pl.BlockSpec((1, r, w), lambda b, pt: (b, 0, 0))
    page_spec = lambda j: pl.BlockSpec((1, 4 * KV_COLS, page), lambda b, pt: (pt[b, j], 0, 0))
    const = lambda a: pl.BlockSpec(a.shape, lambda b, pt: (0,) * a.ndim, pipeline_mode=pl.Buffered(1))
    grid_spec = pltpu.PrefetchScalarGridSpec(
        num_scalar_prefetch=1,
        grid=(nb,),
        in_specs=[seq3(nq, Q_COLS), seq3(nq, 4 * KV_COLS), seq3(nq, 2 * KV_COLS), seq3(nq, 128),
                  pl.BlockSpec((1, 2 * KV_COLS, wb), lambda b, pt: (win_off + b, 0, 0))]
        + [page_spec(j) for j in range(n_pages)]
        + [const(pe4), const(w4), const(cosc), const(sinc), const(expand)],
        out_specs=[seq3(nq, Q_COLS), seq3(2 * KV_COLS, wb)],
        scratch_shapes=[pltpu.VMEM((2, past, KV_COLS), F32)]
        + [pltpu.VMEM((KVH, HD, past), BF16)] * 2
        + [pltpu.VMEM((KVH, N_CMP, HD), BF16)] * 2,
    )
    return pl.pallas_call(
        functools.partial(_attn_sample_kernel, n_pages=n_pages, page=page, past=past),
        grid_spec=grid_spec,
        out_shape=[jax.ShapeDtypeStruct((nb, nq, Q_COLS), F32), jax.ShapeDtypeStruct((nb, 2 * KV_COLS, wb), F32)],
        compiler_params=_params(("arbitrary",)),
        name="attn_sample",
    )(page_ids, q3, kv3, win3, gates3, state_win_t, *([cache_t] * n_pages), pe4, w4, cosc, sinc, expand)


def _conv_kernel(ue_ref, w_ref, b_ref, g_ref, bb_ref, c_ref, *, rows):
    nb, st, ch = c_ref.shape
    pad = CONV_HALO - (CONV_W - 1)
    for r0 in range(0, st, rows):
        acc = jnp.zeros((nb, rows, ch), F32)
        for k in range(CONV_W):
            acc = acc + ue_ref[:, pl.ds(pad + r0 + k, rows), :] * w_ref[k:k + 1, :].reshape(1, 1, ch)
        y = _layer_norm(acc + b_ref[...].reshape(1, 1, ch), g_ref[...].reshape(1, 1, ch), bb_ref[...].reshape(1, 1, ch))
        c_ref[:, r0:r0 + rows, :] = y * _sigmoid(y)


def _conv_module(ue_tiles, w, b, g, bb, nb, st):
    n = ue_tiles.shape[0]
    ch = ue_tiles.shape[2]
    blk = lambda r: pl.BlockSpec((nb, r, ch), lambda i: (i, 0, 0))
    return pl.pallas_call(
        functools.partial(_conv_kernel, rows=min(st, 32)),
        grid=(n // nb,),
        in_specs=[blk(st + CONV_HALO)] + [_const_spec(a.shape) for a in (w, b, g, bb)],
        out_specs=blk(st),
        out_shape=jax.ShapeDtypeStruct((n, st, ch), F32),
        compiler_params=_params(("parallel",)),
        name="conv_module",
    )(ue_tiles, w, b, g, bb)


def _matmul_kernel(x_ref, w_ref, o_ref):
    o_ref[...] = _dot(x_ref[...].astype(BF16), w_ref[...])


def _matmul(x, w, tm):
    t, k = x.shape
    n = w.shape[1]
    return pl.pallas_call(
        _matmul_kernel,
        grid=(t // tm,),
        in_specs=[pl.BlockSpec((tm, k), lambda i: (i, 0)), _const_spec(w.shape)],
        out_specs=pl.BlockSpec((tm, n), lambda i: (i, 0)),
        out_shape=jax.ShapeDtypeStruct((t, n), F32),
        compiler_params=_params(("parallel",)),
        name="pool_inproj",
    )(x, w)


def _pool_kernel(ue_ref, d_ref, *, rows, tiles_per_seq, pos0):
    nb, st, _ = d_ref.shape
    tile = pl.program_id(0) % tiles_per_seq
    for r0 in range(0, st, rows):
        pos = pos0 + tile * st + r0 + lax.broadcasted_iota(jnp.int32, (1, rows, G_C), 1)
        for g, wdw in enumerate(POOL_WINDOWS):
            cols = slice(g * G_C, (g + 1) * G_C)
            acc = ue_ref[:, pl.ds(POOL_HALO + r0, rows), cols]
            u = acc
            for k in range(1, wdw):
                acc = acc + ue_ref[:, pl.ds(POOL_HALO + r0 - k, rows), cols]
            cnt = jnp.minimum(pos + 1, wdw).astype(F32)
            d_ref[:, r0:r0 + rows, cols] = acc / cnt - u


def _pool_diff(ue_tiles, nb, st, tiles_per_seq, pos0):
    n = ue_tiles.shape[0]
    ch = ue_tiles.shape[2]
    blk = lambda r: pl.BlockSpec((nb, r, ch), lambda i: (i, 0, 0))
    return pl.pallas_call(
        functools.partial(_pool_kernel, rows=min(st, 64), tiles_per_seq=tiles_per_seq, pos0=pos0),
        grid=(n // nb,),
        in_specs=[blk(st + POOL_HALO)],
        out_specs=blk(st),
        out_shape=jax.ShapeDtypeStruct((n, st, ch), F32),
        compiler_params=_params(("parallel",)),
        name="pool_diff",
    )(ue_tiles)


def _ffn_ln(x1, w1_ref, w3_ref, w2_ref, g2_ref, b2_ref):
    xb = x1.astype(BF16)
    h1 = _dot(xb, w1_ref[...])
    h3 = _dot(xb, w3_ref[...])
    gated = (h1 * _sigmoid(h1) * h3).astype(BF16)
    return _layer_norm(ALPHA * x1 + _dot(gated, w2_ref[...]), g2_ref[...], b2_ref[...])


def _post_a_kernel(x_ref, oa_ref, c_ref, wo1_ref, wo2_ref, g1_ref, b1_ref, w1_ref, w3_ref, w2_ref, g2_ref, b2_ref,
                   o_ref):
    y = _dot(oa_ref[...].astype(BF16), wo1_ref[...]) + _dot(c_ref[...].astype(BF16), wo2_ref[...])
    x1 = _layer_norm(ALPHA * x_ref[...] + y, g1_ref[...], b1_ref[...])
    o_ref[...] = _ffn_ln(x1, w1_ref, w3_ref, w2_ref, g2_ref, b2_ref)


def _post_c_kernel(x_ref, d_ref, wg_ref, sc_ref, wo_ref, g1_ref, b1_ref, w1_ref, w3_ref, w2_ref, g2_ref, b2_ref,
                   o_ref):
    db = d_ref[...].astype(BF16)
    z = jnp.concatenate([_dot(db[:, g * G_C:(g + 1) * G_C], wg_ref[g]) for g in range(len(POOL_WINDOWS))], axis=1)
    y = _dot((z * sc_ref[...]).astype(BF16), wo_ref[...])
    x1 = _layer_norm(ALPHA * x_ref[...] + y, g1_ref[...], b1_ref[...])
    o_ref[...] = _ffn_ln(x1, w1_ref, w3_ref, w2_ref, g2_ref, b2_ref)


def _post(body, x, acts, consts, tm):
    t = x.shape[0]
    row = lambda a: pl.BlockSpec((tm, a.shape[1]), lambda i: (i, 0))
    return pl.pallas_call(
        body,
        grid=(t // tm,),
        in_specs=[row(x)] + [row(a) for a in acts] + [_const_spec(c.shape) for c in consts],
        out_specs=row(x),
        out_shape=jax.ShapeDtypeStruct(x.shape, F32),
        compiler_params=_params(("parallel",)),
        name=body.__name__.strip("_"),
    )(x, *acts, *consts)


def _rope_tables(pos):
    half = HD // 2
    inv = ROPE_THETA ** (-jnp.arange(half, dtype=F32) / half)
    ang = pos.astype(F32)[:, None] * inv[None, :]
    cos, sin = jnp.cos(ang), jnp.sin(ang)
    return jnp.tile(cos, (1, 4)), jnp.tile(jnp.concatenate([-sin, sin], axis=1), (1, 2))


def _halo_tiles(u3, hist, halo, ts):
    b, s, c = u3.shape
    if hist is None:
        ue = jnp.pad(u3, ((0, 0), (halo, 0), (0, 0)))
    else:
        ue = jnp.concatenate([jnp.zeros((b, halo - hist.shape[1], c), u3.dtype), hist, u3], axis=1)
    nt = s // ts
    if nt == 1:
        return ue
    return jnp.stack([ue[:, i * ts:i * ts + ts + halo] for i in range(nt)], axis=1).reshape(b * nt, ts + halo, c)


def _prep_weights(p):
    n_a = p["w_in_a"].shape[0]
    cuts = np.cumsum([0, Q_COLS, 4 * KV_COLS, 2 * KV_COLS, GATE_COLS, C_B, C_B])
    out = {"a": [], "c": [], "ffn": []}
    for ia in range(n_a):
        w_in = p["w_in_a"][ia]
        wq, wkv, wwin, wgl, wga, wgb = [w_in[:, cuts[k]:cuts[k + 1]].astype(BF16) for k in range(6)]
        wglt = jnp.pad(wgl.T, ((0, GATE_ROWS - GATE_COLS), (0, 0)))
        wglr = jnp.pad(wgl, ((0, 0), (0, 128 - GATE_COLS)))
        eye = jnp.eye(4, dtype=F32)
        wk, wv = p["w_cmp_k"][ia], p["w_cmp_v"][ia]
        blocks = jnp.stack([wk, wk, wv, wv], axis=1)
        w4 = jnp.einsum("pkde,kl->pkdle", blocks, eye).reshape(L_CMP, 4 * HD, 4 * HD).astype(BF16)
        pe4 = jnp.concatenate([p["pe_cmp_k"][ia]] * 2 + [p["pe_cmp_v"][ia]] * 2, axis=1)
        wo = p["w_out_a"][ia].astype(BF16)
        out["a"].append(dict(
            inproj_t=(wq, wkv, wwin, wga, wgb, wglt), inproj_r=(wq, wkv, wwin, wga, wgb, wglr), w4=w4, pe4=pe4,
            conv=(jnp.pad(p["conv_w"][ia], ((0, 1), (0, 0))), p["conv_b"][ia][None], p["conv_ln_g"][ia][None],
                  p["conv_ln_b"][ia][None]),
            wo1=wo[:Q_COLS], wo2=wo[Q_COLS:]))
    for ic in range(p["w_in_c"].shape[0]):
        out["c"].append(dict(w_in=p["w_in_c"][ic].astype(BF16), w_grp=p["w_grp_c"][ic].astype(BF16),
                             scale=p["scale_c"][ic][None], w_out=p["w_out_c"][ic].astype(BF16)))
    for l in range(DEPTH):
        out["ffn"].append((p["ln1_g"][l][None], p["ln1_b"][l][None], p["w1"][l].astype(BF16),
                           p["w3"][l].astype(BF16), p["w2"][l].astype(BF16), p["ln2_g"][l][None], p["ln2_b"][l][None]))
    return out


def _trunk(x3, pos0, wb, wts, cache_kv, page_table, state_win, state_conv, state_pool):
    b, s, _ = x3.shape
    t = b * s
    prompt = cache_kv is None
    tm = min(256, t)
    x = x3.reshape(t, D_MODEL)
    cos, sin = _rope_tables(pos0 + jnp.arange(s, dtype=jnp.int32))
    if not prompt:
        cos, sin = jnp.tile(cos, (tm // s, 1)), jnp.tile(sin, (tm // s, 1))
    cosc, sinc = _rope_tables(jnp.arange(N_CMP, dtype=jnp.int32) * L_CMP + (L_CMP - 1))
    ts = 512 if prompt else s
    nb = 1 if prompt else 8
    if not prompt:
        n_a, n_pool, page = cache_kv.shape[:3]
        cache_t = jnp.transpose(cache_kv, (0, 1, 3, 4, 5, 2)).reshape(n_a * n_pool, 4 * KV_COLS, page)
        sw_t = jnp.transpose(state_win, (0, 1, 3, 4, 5, 2)).reshape(n_a * b, 2 * KV_COLS, wb)
        lane_blk = jnp.arange(128)[:, None]
        key_blk = jnp.arange(page_table.shape[1] * page + 128)[None, :] // L_SEL
        expand = ((lane_blk % 2 == 0) & (lane_blk // 2 == key_blk)).astype(BF16)
    kv_new, win_new, conv_new, pool_new = [], [], [], []
    for l in range(DEPTH):
        ln1_g, ln1_b, w1, w3, w2, ln2_g, ln2_b = wts["ffn"][l]
        if l % 2 == 0:
            a = wts["a"][l // 2]
            q, kv, win, u, gates = _inproj_a(x, a["inproj_t" if prompt else "inproj_r"], cos, sin, tm, prompt)
            kv_new.append(kv.reshape(b, s, 4, KVH, HD))
            if prompt:
                oa = _attn_prompt(q, kv, win, gates, a["pe4"], a["w4"], cosc, sinc, b, s)
                win_new.append(win.reshape(b, s, 2, KVH, HD)[:, s - wb:])
                hist = None
            else:
                ia = l // 2
                n_pool = cache_kv.shape[1]
                oa, wout = _attn_sample(page_table + ia * n_pool, ia * b, q.reshape(b, s, Q_COLS),
                                        kv.reshape(b, s, 4 * KV_COLS), win.reshape(b, s, 2 * KV_COLS),
                                        gates.reshape(b, s, 128), sw_t, cache_t, a["pe4"], a["w4"], cosc, sinc, expand)
                oa = oa.reshape(t, Q_COLS)
                win_new.append(wout.reshape(b, 2, KVH, HD, wb).transpose(0, 4, 1, 2, 3))
                hist = state_conv[ia]
            u3 = u.reshape(b, s, C_B)
            ue = _halo_tiles(u3, hist, CONV_HALO, ts)
            c = _conv_module(ue, *a["conv"], nb, ts).reshape(t, C_B)
            keep = CONV_W - 1
            conv_new.append(u3[:, s - keep:] if hist is None else jnp.concatenate([hist, u3], axis=1)[:, -keep:])
            x = _post(_post_a_kernel, x, (oa, c), (a["wo1"], a["wo2"], ln1_g, ln1_b, w1, w3, w2, ln2_g, ln2_b), tm)
        else:
            cw = wts["c"][l // 2]
            u3 = _matmul(x, cw["w_in"], tm).reshape(b, s, D_MODEL)
            hist = None if prompt else state_pool[l // 2]
            ue = _halo_tiles(u3, hist, POOL_HALO, ts)
            d = _pool_diff(ue, nb, ts, s // ts, pos0).reshape(t, D_MODEL)
            pool_new.append(u3[:, s - POOL_HIST:] if hist is None
                            else jnp.concatenate([hist, u3], axis=1)[:, -POOL_HIST:])
            x = _post(_post_c_kernel, x, (d,),
                      (cw["w_grp"], cw["scale"], cw["w_out"], ln1_g, ln1_b, w1, w3, w2, ln2_g, ln2_b), tm)
    return x.reshape(b, s, D_MODEL), jnp.stack(kv_new), jnp.stack(win_new), jnp.stack(conv_new), jnp.stack(pool_new)


def kernel(x_prompt, x_sample, cache_kv, state_win, state_conv, state_pool, page_table, w_in_a, w_cmp_k, pe_cmp_k, w_cmp_v, pe_cmp_v, conv_w, conv_b, conv_ln_g, conv_ln_b, w_out_a, w_in_c, w_grp_c, scale_c, w_out_c, ln1_g, ln1_b, ln2_g, ln2_b, w1, w3, w2):
    p = dict(w_in_a=w_in_a, w_cmp_k=w_cmp_k, pe_cmp_k=pe_cmp_k, w_cmp_v=w_cmp_v, pe_cmp_v=pe_cmp_v,
             conv_w=conv_w, conv_b=conv_b, conv_ln_g=conv_ln_g, conv_ln_b=conv_ln_b, w_out_a=w_out_a,
             w_in_c=w_in_c, w_grp_c=w_grp_c, scale_c=scale_c, w_out_c=w_out_c,
             ln1_g=ln1_g, ln1_b=ln1_b, ln2_g=ln2_g, ln2_b=ln2_b, w1=w1, w3=w3, w2=w2)
    wts = _prep_weights(p)
    past_len = page_table.shape[1] * cache_kv.shape[2]
    wb = state_win.shape[2]
    y_p, kv_p, win_p, conv_p, pool_p = _trunk(x_prompt, 0, wb, wts, None, None, None, None, None)
    y_s, kv_s, win_s, conv_s, pool_s = _trunk(x_sample, past_len, wb, wts, cache_kv, page_table, state_win,
                                              state_conv, state_pool)
    return (y_p, y_s, kv_p, kv_s, win_p, win_s, conv_p, conv_s, pool_p, pool_s)
```

```python
import functools

import numpy as np
import jax
import jax.numpy as jnp
from jax import lax
from jax.experimental import pallas as pl
from jax.experimental.pallas import tpu as pltpu

F32 = jnp.float32
BF16 = jnp.bfloat16

D_MODEL = 1024
DEPTH = 4
HD = 64
H_A = 8
KVH = 2
HPG = H_A // KVH
L_CMP = 32
L_SEL = 64
N_SEL = 8
WINDOW = 512
ROPE_THETA = 10000.0
FORCE_BONUS = 1.0e4
NEG_INF = -1.0e30
TINY = 1.0e-30
C_B = D_MODEL // 2
CONV_W = 31
POOL_WINDOWS = (2, 4, 8, 16)
G_C = D_MODEL // len(POOL_WINDOWS)
POOL_HIST = max(POOL_WINDOWS) - 1
D_FF = 2816
ALPHA = (2 * DEPTH) ** 0.25
Q_COLS = H_A * HD
KV_COLS = KVH * HD
GATE_COLS = 3 * H_A
GATE_ROWS = 32
Q_BLOCK = 128
N_CMP = 64
SEL_CHUNK = 256
WIN_SPAN = WINDOW + Q_BLOCK
CONV_HALO = 32
POOL_HALO = 16
VMEM_LIMIT = 56 * 1024 * 1024


def _dot(a, b):
    return jnp.dot(a, b, preferred_element_type=F32)


def _dot_nt(a, b):
    return lax.dot_general(a, b, (((1,), (1,)), ((), ())), preferred_element_type=F32)


def _dot_tn(a, b):
    return lax.dot_general(a, b, (((0,), (0,)), ((), ())), preferred_element_type=F32)


def _sigmoid(x):
    return 1.0 / (1.0 + jnp.exp(-x))


def _layer_norm(y, g, b):
    mu = jnp.mean(y, axis=-1, keepdims=True)
    d = y - mu
    var = jnp.mean(d * d, axis=-1, keepdims=True)
    return d * lax.rsqrt(var + 1e-5) * g + b


def _rope128(v, cos, sin_signed):
    lane = lax.broadcasted_iota(jnp.int32, v.shape, 1)
    rot = jnp.where((lane % HD) < HD // 2, pltpu.roll(v, 128 - HD // 2, 1), pltpu.roll(v, HD // 2, 1))
    return v * cos + rot * sin_signed


def _params(sem=None):
    return pltpu.CompilerParams(dimension_semantics=sem, vmem_limit_bytes=VMEM_LIMIT)


def _const_spec(shape):
    nd = len(shape)
    return pl.BlockSpec(shape, lambda *_: (0,) * nd, pipeline_mode=pl.Buffered(1))


def _inproj_a_kernel(x_ref, wq_ref, wkv_ref, wwin_ref, wga_ref, wgb_ref, wgl_ref, cos_ref, sin_ref,
                     q_out, kv_out, win_out, u_out, g_out, *, gates_transposed):
    xb = x_ref[...].astype(BF16)
    cos = cos_ref[...]
    sin = sin_ref[...]
    q = _dot(xb, wq_ref[...])
    for c in range(Q_COLS // 128):
        q_out[:, c * 128:(c + 1) * 128] = _rope128(q[:, c * 128:(c + 1) * 128], cos, sin) * (HD ** -0.5)
    kv = _dot(xb, wkv_ref[...])
    kv_out[:, 0:256] = kv[:, 0:256]
    kv_out[:, 256:384] = _rope128(kv[:, 256:384], cos, sin)
    kv_out[:, 384:512] = kv[:, 384:512]
    w = _dot(xb, wwin_ref[...])
    win_out[:, 0:128] = _rope128(w[:, 0:128], cos, sin)
    win_out[:, 128:256] = w[:, 128:256]
    u_out[...] = _dot(xb, wga_ref[...]) * _sigmoid(_dot(xb, wgb_ref[...]))
    g_out[...] = _sigmoid(_dot_nt(wgl_ref[...], xb) if gates_transposed else _dot(xb, wgl_ref[...]))


def _inproj_a(x, wts, cos, sin, tm, gates_transposed):
    t = x.shape[0]
    g_spec = pl.BlockSpec((GATE_ROWS, tm), lambda i: (0, i)) if gates_transposed else pl.BlockSpec((tm, 128), lambda i: (i, 0))
    g_shape = (GATE_ROWS, t) if gates_transposed else (t, 128)
    n_tab = cos.shape[0] // tm
    row = lambda w: pl.BlockSpec((tm, w), lambda i: (i, 0))
    tab = pl.BlockSpec((tm, 128), lambda i: (i % n_tab, 0))
    return pl.pallas_call(
        functools.partial(_inproj_a_kernel, gates_transposed=gates_transposed),
        grid=(t // tm,),
        in_specs=[row(D_MODEL)] + [_const_spec(w.shape) for w in wts] + [tab, tab],
        out_specs=[row(Q_COLS), row(4 * KV_COLS), row(2 * KV_COLS), row(C_B), g_spec],
        out_shape=[jax.ShapeDtypeStruct((t, Q_COLS), F32), jax.ShapeDtypeStruct((t, 4 * KV_COLS), F32),
                   jax.ShapeDtypeStruct((t, 2 * KV_COLS), F32), jax.ShapeDtypeStruct((t, C_B), F32),
                   jax.ShapeDtypeStruct(g_shape, F32)],
        compiler_params=_params(("parallel",)),
        name="inproj_a",
    )(x, *wts, cos, sin)


def _compress(rows_at, pe4_ref, w4_ref, cosc_ref, sinc_ref):
    acc = jnp.zeros((N_CMP, 4 * HD), F32)
    for p in range(L_CMP):
        xp = rows_at(p) + pe4_ref[p:p + 1, :]
        acc = acc + _dot(xp.astype(BF16), w4_ref[p])
    k = _rope128(acc[:, 0:2 * HD], cosc_ref[...], sinc_ref[...])
    v = acc[:, 2 * HD:4 * HD]
    return ([k[:, g * HD:(g + 1) * HD].astype(BF16) for g in range(KVH)],
            [v[:, g * HD:(g + 1) * HD].astype(BF16) for g in range(KVH)])


def _softmax_t(s, msk):
    s = jnp.where(msk, s, NEG_INF)
    m = jnp.max(s, axis=0, keepdims=True)
    p = jnp.where(msk, jnp.exp(s - m), 0.0)
    return m, p, jnp.sum(p, axis=0, keepdims=True)


def _nsa_core_t(qf, gt, t_q, t0, kcmp, vcmp, ksel, vselt, kwin, vwint, imp_scr, bias_scr, n_sel_blocks):
    nq = Q_BLOCK
    w = HPG * nq
    groups = range(KVH)
    t_lane = jnp.concatenate([t_q] * HPG, axis=1)
    row_c = lax.broadcasted_iota(jnp.int32, (N_CMP, w), 0)
    row_s = lax.broadcasted_iota(jnp.int32, (SEL_CHUNK, w), 0)
    row_q = lax.broadcasted_iota(jnp.int32, (nq, w), 0)
    sel_per_chunk = SEL_CHUNK // L_SEL
    n_full = t0 // SEL_CHUNK
    win_lo = pl.multiple_of(jnp.maximum(t0 + nq - WIN_SPAN, 0), nq)
    qgs = [jnp.concatenate([qf[:, (g * HPG + h) * HD:(g * HPG + h + 1) * HD] for h in range(HPG)],
                           axis=0).astype(BF16) for g in groups]
    o_c = []
    for g in groups:
        m_c = (row_c * L_CMP + (L_CMP - 1)) <= t_lane
        _, e_c, l_c = _softmax_t(_dot_nt(kcmp[g], qgs[g]), m_c)
        p_c = e_c / jnp.maximum(l_c, TINY)
        o_c.append(_dot_tn(vcmp[g], p_c.astype(BF16)))
        imp = p_c[:, 0:nq]
        for h in range(1, HPG):
            imp = imp + p_c[:, h * nq:(h + 1) * nq]
        imp_scr[g] = imp
        half = N_CMP // 2
        imp2 = imp_scr[g, pl.ds(0, half, stride=2), :] + imp_scr[g, pl.ds(1, half, stride=2), :]
        blk = lax.broadcasted_iota(jnp.int32, (n_sel_blocks, nq), 0)
        cur = t_q // L_SEL
        forced = (blk == 0) | (blk == cur) | (blk == cur - 1)
        valid = (blk * L_SEL) <= t_q
        score = jnp.where(valid, imp2 + jnp.where(forced, FORCE_BONUS, 0.0), -FORCE_BONUS)
        cnt = jnp.zeros((n_sel_blocks, nq), jnp.int32)
        for k in range(n_sel_blocks):
            rk = score[k:k + 1, :]
            beats = (rk > score) | ((rk == score) & (blk > k))
            cnt = cnt + beats.astype(jnp.int32)
        bias_scr[g] = jnp.concatenate([jnp.where(cnt < N_SEL, 0.0, NEG_INF)] * HPG, axis=1)

    def sel_chunk(j, carries, diagonal):
        off = pl.multiple_of(j * SEL_CHUNK, SEL_CHUNK)
        out = []
        for g in groups:
            m, l, acc = carries[g]
            s = _dot_nt(ksel[g, pl.ds(off, SEL_CHUNK), :], qgs[g])
            s = jnp.concatenate([s[r * L_SEL:(r + 1) * L_SEL] + bias_scr[g, pl.ds(j * sel_per_chunk + r, 1), :]
                                 for r in range(sel_per_chunk)], axis=0)
            if diagonal:
                s = jnp.where((off + row_s) <= t_lane, s, NEG_INF)
            m_j = jnp.max(s, axis=0, keepdims=True)
            p = jnp.exp(s - m_j)
            m_new = jnp.maximum(m, m_j)
            a_old = jnp.exp(m - m_new)
            a_new = jnp.exp(m_j - m_new)
            pv = _dot(vselt[g, :, pl.ds(off, SEL_CHUNK)], p.astype(BF16))
            out.append((m_new, a_old * l + a_new * jnp.sum(p, axis=0, keepdims=True), a_old * acc + a_new * pv))
        return tuple(out)

    init = tuple((jnp.full((1, w), NEG_INF, F32), jnp.zeros((1, w), F32), jnp.zeros((HD, w), F32)) for _ in groups)
    carries = lax.fori_loop(0, n_full, functools.partial(sel_chunk, diagonal=False), init)
    carries = sel_chunk(n_full, carries, diagonal=True)
    outs = []
    for g in groups:
        _, l_s, a_s = carries[g]
        o_s = a_s / l_s
        s_w = _dot_nt(kwin[g, pl.ds(win_lo, WIN_SPAN), :], qgs[g])
        pieces = []
        for k in range(WIN_SPAN // nq):
            kpos = win_lo + k * nq + row_q
            ok = kpos <= t_lane
            if k == 0:
                ok = ok & (kpos > t_lane - WINDOW)
            pieces.append(jnp.where(ok, s_w[k * nq:(k + 1) * nq], NEG_INF))
        s_w = jnp.concatenate(pieces, axis=0)
        p_w = jnp.exp(s_w - jnp.max(s_w, axis=0, keepdims=True))
        o_w = (_dot(vwint[g, :, pl.ds(win_lo, WIN_SPAN)], p_w.astype(BF16))
               / jnp.sum(p_w, axis=0, keepdims=True))
        for h in range(HPG):
            r = 3 * (g * HPG + h)
            sl = slice(h * nq, (h + 1) * nq)
            outs.append(gt[r:r + 1, :] * o_c[g][:, sl] + gt[r + 1:r + 2, :] * o_s[:, sl]
                        + gt[r + 2:r + 3, :] * o_w[:, sl])
    return outs


def _attn_prompt_kernel(q_ref, kv_ref, win_ref, gt_ref, pe4_ref, w4_ref, cosc_ref, sinc_ref, o_ref,
                        kcf, ksel, vselt, kwin, vwint, kcmp, vcmp, imp_scr, bias_scr, *, seq):
    i = pl.program_id(1)

    @pl.when(i == 0)
    def _():
        v_t = kv_ref[:, 3 * KV_COLS:4 * KV_COLS].T.astype(BF16)
        vw_t = win_ref[:, KV_COLS:2 * KV_COLS].T.astype(BF16)
        for g in range(KVH):
            ksel[g] = kv_ref[:, 2 * KV_COLS + g * HD:2 * KV_COLS + (g + 1) * HD].astype(BF16)
            kwin[g] = win_ref[:, g * HD:(g + 1) * HD].astype(BF16)
            vselt[g] = v_t[g * HD:(g + 1) * HD]
            vwint[g] = vw_t[g * HD:(g + 1) * HD]
        for c in range(2):
            kcf[c] = kv_ref[:, c * KV_COLS:(c + 1) * KV_COLS]
        rows_at = lambda p: jnp.concatenate([kcf[c, pl.ds(p, N_CMP, stride=L_CMP), :] for c in range(2)], axis=1)
        k_c, v_c = _compress(rows_at, pe4_ref, w4_ref, cosc_ref, sinc_ref)
        for g in range(KVH):
            kcmp[g] = k_c[g]
            vcmp[g] = v_c[g]

    t0 = i * Q_BLOCK
    t_q = t0 + lax.broadcasted_iota(jnp.int32, (1, Q_BLOCK), 1)
    outs = _nsa_core_t(q_ref[...], gt_ref[...], t_q, t0, kcmp, vcmp, ksel, vselt, kwin, vwint, imp_scr, bias_scr,
                       n_sel_blocks=seq // L_SEL)
    for pair in range(H_A // 2):
        o_ref[:, pair * 128:(pair + 1) * 128] = jnp.concatenate(outs[2 * pair:2 * pair + 2], axis=0).T


def _attn_prompt(q, kv, win, gt, pe4, w4, cosc, sinc, batch, seq):
    t = q.shape[0]
    nqb = seq // Q_BLOCK
    qrow = lambda w: pl.BlockSpec((Q_BLOCK, w), lambda b, i: (b * nqb + i, 0))
    srow = lambda w: pl.BlockSpec((seq, w), lambda b, i: (b, 0))
    return pl.pallas_call(
        functools.partial(_attn_prompt_kernel, seq=seq),
        grid=(batch, nqb),
        in_specs=[qrow(Q_COLS), srow(4 * KV_COLS), srow(2 * KV_COLS),
                  pl.BlockSpec((GATE_ROWS, Q_BLOCK), lambda b, i: (0, b * nqb + i)),
                  _const_spec(pe4.shape), _const_spec(w4.shape), _const_spec(cosc.shape), _const_spec(sinc.shape)],
        out_specs=qrow(Q_COLS),
        out_shape=jax.ShapeDtypeStruct((t, Q_COLS), F32),
        scratch_shapes=[pltpu.VMEM((2, seq, KV_COLS), F32)]
        + [pltpu.VMEM((KVH, seq, HD), BF16), pltpu.VMEM((KVH, HD, seq), BF16)] * 2
        + [pltpu.VMEM((KVH, N_CMP, HD), BF16)] * 2
        + [pltpu.VMEM((KVH, N_CMP, Q_BLOCK), F32), pltpu.VMEM((KVH, seq // L_SEL, HPG * Q_BLOCK), F32)],
        compiler_params=_params(("parallel", "arbitrary")),
        name="attn_prompt",
    )(q, kv, win, gt, pe4, w4, cosc, sinc)


def _softmax_rows(parts):
    masked = [jnp.where(k, s, NEG_INF) for s, k in parts]
    m = masked[0].max(axis=1, keepdims=True)
    for s in masked[1:]:
        m = jnp.maximum(m, s.max(axis=1, keepdims=True))
    es = [jnp.where(k, jnp.exp(s - m), 0.0) for s, (_, k) in zip(masked, parts)]
    l = es[0].sum(axis=1, keepdims=True)
    for e in es[1:]:
        l = l + e.sum(axis=1, keepdims=True)
    return es, jnp.maximum(l, TINY)


def _attn_sample_kernel(pt_ref, q_ref, kv_ref, win_ref, gate_ref, sw_ref, *rest, n_pages, page, past):
    del pt_ref
    pages = rest[:n_pages]
    pe4_ref, w4_ref, cosc_ref, sinc_ref, expand_ref, perm_ref, o_ref, wout_ref = rest[n_pages:n_pages + 8]
    nq = q_ref.shape[1]
    wb = sw_ref.shape[2]
    new_pad = 16
    group = perm_ref.shape[0] // page
    moved = []
    for jj in range(n_pages // group):
        x_t = jnp.concatenate([pages[group * jj + r][0, 0:2 * KV_COLS, :] for r in range(group)], axis=1)
        hi = x_t.astype(BF16)
        lo = (x_t - hi.astype(F32)).astype(BF16)
        moved.append(_dot_nt(perm_ref[...], hi) + _dot_nt(perm_ref[...], lo))
    rows_at = lambda p: jnp.concatenate([m[8 * p:8 * p + 8] for m in moved], axis=0)
    kcmp, vcmp = _compress(rows_at, pe4_ref, w4_ref, cosc_ref, sinc_ref)
    cached = lambda r0: jnp.concatenate([pages[j][0, r0:r0 + HD, :].astype(BF16) for j in range(n_pages)], axis=1)
    kt = [cached(2 * KV_COLS + g * HD) for g in range(KVH)]
    vt = [cached(3 * KV_COLS + g * HD) for g in range(KVH)]

    qf = q_ref[0]
    kvn = kv_ref[0]
    wn = win_ref[0]
    gates = gate_ref[0]
    wout_ref[0, :, 0:wb - nq] = sw_ref[0, :, nq:wb]
    wout_ref[0, :, wb - nq:wb] = wn.T

    groups = range(KVH)
    rows = H_A * nq
    grp = lambda a, g: a[g * HPG * nq:(g + 1) * HPG * nq]
    per_group = lambda f: jnp.concatenate([f(g) for g in groups], axis=0)
    zpad = jnp.zeros((new_pad - nq, HD), F32)
    new_rows = lambda a: jnp.concatenate([a, zpad], axis=0).astype(BF16)
    ns_real = past // L_SEL + 1
    iq = lambda n: lax.broadcasted_iota(jnp.int32, (rows, n), 0) % nq
    ik = lambda n: lax.broadcasted_iota(jnp.int32, (rows, n), 1)
    causal_new = (ik(new_pad) <= iq(new_pad)) & (ik(new_pad) < nq)
    qg = [jnp.concatenate([qf[:, (g * HPG + h) * HD:(g * HPG + h + 1) * HD] for h in range(HPG)],
                          axis=0).astype(BF16) for g in groups]
    m_c = (ik(N_CMP) * L_CMP + (L_CMP - 1)) <= past + iq(N_CMP)
    (e_c,), l_c = _softmax_rows([(per_group(lambda g: _dot_nt(qg[g], kcmp[g])), m_c)])
    p_c = e_c / l_c
    o_c = per_group(lambda g: _dot(grp(p_c, g).astype(BF16), vcmp[g]))
    imp = per_group(lambda g: sum(p_c[(g * HPG + h) * nq:(g * HPG + h + 1) * nq] for h in range(HPG)))
    imp = jnp.concatenate([imp, jnp.zeros((KVH * nq, 128 - N_CMP), F32)], axis=1)
    imp2 = imp + pltpu.roll(imp, 127, 1)
    lane = lax.broadcasted_iota(jnp.int32, (KVH * nq, 128), 1)
    blk = lane // 2
    t_sel = past + lax.broadcasted_iota(jnp.int32, (KVH * nq, 128), 0) % nq
    cur = t_sel // L_SEL
    forced = (blk == 0) | (blk == cur) | (blk == cur - 1)
    valid = (blk * L_SEL) <= t_sel
    score = jnp.where(valid, imp2 + jnp.where(forced, FORCE_BONUS, 0.0), -FORCE_BONUS)
    score = jnp.where(((lane & 1) == 0) & (blk < ns_real), score, -jnp.inf)
    cnt = jnp.zeros((KVH * nq, 128), jnp.int32)
    for k in range(ns_real):
        ck = score[:, 2 * k:2 * k + 1]
        beats = (ck > score) | ((ck == score) & (lane > 2 * k))
        cnt = cnt + beats.astype(jnp.int32)
    sel = (cnt < N_SEL).astype(BF16)
    sel_rows = per_group(lambda g: jnp.concatenate([sel[g * nq:(g + 1) * nq]] * HPG, axis=0))
    picked = _dot(sel_rows, expand_ref[...]) > 0.5
    s_p = per_group(lambda g: _dot(qg[g], kt[g]))
    s_n = per_group(lambda g: _dot_nt(qg[g], new_rows(kvn[:, 2 * KV_COLS + g * HD:2 * KV_COLS + (g + 1) * HD])))
    (e_p, e_n), l_s = _softmax_rows([(s_p, picked[:, 0:past]), (s_n, picked[:, past:past + new_pad] & causal_new)])
    o_s = per_group(lambda g: _dot_nt(grp(e_p, g).astype(BF16), vt[g]) + _dot(
        grp(e_n, g).astype(BF16), new_rows(kvn[:, 3 * KV_COLS + g * HD:3 * KV_COLS + (g + 1) * HD]))) / l_s
    s_wp = per_group(lambda g: _dot(qg[g], sw_ref[0, g * HD:(g + 1) * HD, :].astype(BF16)))
    s_wn = per_group(lambda g: _dot_nt(qg[g], new_rows(wn[:, g * HD:(g + 1) * HD])))
    (e_wp, e_wn), l_w = _softmax_rows([(s_wp, ik(wb) > iq(wb) + (wb - WINDOW)), (s_wn, causal_new)])
    o_w = per_group(lambda g: _dot_nt(
        grp(e_wp, g).astype(BF16), sw_ref[0, KV_COLS + g * HD:KV_COLS + (g + 1) * HD, :].astype(BF16)) + _dot(
        grp(e_wn, g).astype(BF16), new_rows(wn[:, KV_COLS + g * HD:KV_COLS + (g + 1) * HD]))) / l_w
    gate = lambda br: jnp.concatenate([gates[:, 3 * hd + br:3 * hd + br + 1] for hd in range(H_A)], axis=0)
    o = gate(0) * o_c + gate(1) * o_s + gate(2) * o_w
    for pair in range(H_A // 2):
        o_ref[0, :, 2 * pair * HD:(2 * pair + 2) * HD] = jnp.concatenate(
            [o[2 * pair * nq:(2 * pair + 1) * nq], o[(2 * pair + 1) * nq:(2 * pair + 2) * nq]], axis=1)


def _attn_sample(page_ids, win_off, q3, kv3, win3, gates3, state_win_t, cache_t, pe4, w4, cosc, sinc, expand, perm):
    nb, nq, _ = q3.shape
    n_pages = page_ids.shape[1]
    page = cache_t.shape[2]
    past = n_pages * page
    wb = state_win_t.shape[2]
    seq3 = lambda r, w: pl.BlockSpec((1, r, w), lambda b, pt: (b, 0, 0))
    page_spec = lambda j: pl.BlockSpec((1, 4 * KV_COLS, page), lambda b, pt: (pt[b, j], 0, 0))
    const = lambda a: pl.BlockSpec(a.shape, lambda b, pt: (0,) * a.ndim, pipeline_mode=pl.Buffered(1))
    grid_spec = pltpu.PrefetchScalarGridSpec(
        num_scalar_prefetch=1,
        grid=(nb,),
        in_specs=[seq3(nq, Q_COLS), seq3(nq, 4 * KV_COLS), seq3(nq, 2 * KV_COLS), seq3(nq, 128),
                  pl.BlockSpec((1, 2 * KV_COLS, wb), lambda b, pt: (win_off + b, 0, 0))]
        + [page_spec(j) for j in range(n_pages)]
        + [const(pe4), const(w4), const(cosc), const(sinc), const(expand), const(perm)],
        out_specs=[seq3(nq, Q_COLS), seq3(2 * KV_COLS, wb)],
    )
    return pl.pallas_call(
        functools.partial(_attn_sample_kernel, n_pages=n_pages, page=page, past=past),
        grid_spec=grid_spec,
        out_shape=[jax.ShapeDtypeStruct((nb, nq, Q_COLS), F32), jax.ShapeDtypeStruct((nb, 2 * KV_COLS, wb), F32)],
        compiler_params=_params(("arbitrary",)),
        name="attn_sample",
    )(page_ids, q3, kv3, win3, gates3, state_win_t, *([cache_t] * n_pages), pe4, w4, cosc, sinc, expand, perm)


def _conv_kernel(ue_ref, w_ref, b_ref, g_ref, bb_ref, c_ref, *, rows):
    nb, st, ch = c_ref.shape
    pad = CONV_HALO - (CONV_W - 1)
    for r0 in range(0, st, rows):
        acc = jnp.zeros((nb, rows, ch), F32)
        for k in range(CONV_W):
            acc = acc + ue_ref[:, pl.ds(pad + r0 + k, rows), :] * w_ref[k:k + 1, :].reshape(1, 1, ch)
        y = _layer_norm(acc + b_ref[...].reshape(1, 1, ch), g_ref[...].reshape(1, 1, ch), bb_ref[...].reshape(1, 1, ch))
        c_ref[:, r0:r0 + rows, :] = y * _sigmoid(y)


def _with_history(body, halo):
    def kernel_fn(cur_ref, prev_ref, *rest):
        *refs, ue_scr = rest
        first = pl.program_id(1) == 0
        ue_scr[:, 0:halo, :] = jnp.where(first, 0.0, prev_ref[...])
        ue_scr[:, halo:, :] = cur_ref[...]
        body(ue_scr, *refs)
    return kernel_fn


def _halo_specs(ts, halo, ch):
    cur = pl.BlockSpec((1, ts, ch), lambda b, i: (b, i, 0))
    prev = pl.BlockSpec((1, halo, ch), lambda b, i: (b, jnp.maximum(i * (ts // halo) - 1, 0), 0))
    return cur, prev


def _conv_module_seq(u3, w, b, g, bb, ts):
    nseq, s, ch = u3.shape
    cur, prev = _halo_specs(ts, CONV_HALO, ch)
    consts = (w, b, g, bb)
    return pl.pallas_call(
        _with_history(functools.partial(_conv_kernel, rows=32), CONV_HALO),
        grid=(nseq, s // ts),
        in_specs=[cur, prev] + [pl.BlockSpec(a.shape, lambda bi, i: (0, 0), pipeline_mode=pl.Buffered(1)) for a in consts],
        out_specs=cur,
        out_shape=jax.ShapeDtypeStruct((nseq, s, ch), F32),
        scratch_shapes=[pltpu.VMEM((1, ts + CONV_HALO, ch), F32)],
        compiler_params=_params(("parallel", "arbitrary")),
        name="conv_module_seq",
    )(u3, u3, *consts)


def _conv_module(ue_tiles, w, b, g, bb, nb, st):
    n = ue_tiles.shape[0]
    ch = ue_tiles.shape[2]
    blk = lambda r: pl.BlockSpec((nb, r, ch), lambda i: (i, 0, 0))
    return pl.pallas_call(
        functools.partial(_conv_kernel, rows=min(st, 32)),
        grid=(n // nb,),
        in_specs=[blk(st + CONV_HALO)] + [_const_spec(a.shape) for a in (w, b, g, bb)],
        out_specs=blk(st),
        out_shape=jax.ShapeDtypeStruct((n, st, ch), F32),
        compiler_params=_params(("parallel",)),
        name="conv_module",
    )(ue_tiles, w, b, g, bb)


def _matmul_kernel(x_ref, w_ref, o_ref):
    o_ref[...] = _dot(x_ref[...].astype(BF16), w_ref[...])


def _matmul(x, w, tm):
    t, k = x.shape
    n = w.shape[1]
    return pl.pallas_call(
        _matmul_kernel,
        grid=(t // tm,),
        in_specs=[pl.BlockSpec((tm, k), lambda i: (i, 0)), _const_spec(w.shape)],
        out_specs=pl.BlockSpec((tm, n), lambda i: (i, 0)),
        out_shape=jax.ShapeDtypeStruct((t, n), F32),
        compiler_params=_params(("parallel",)),
        name="pool_inproj",
    )(x, w)


def _pool_kernel(ue_ref, d_ref, *, rows, tile_axis, pos0):
    nb, st, _ = d_ref.shape
    tile = 0 if tile_axis is None else pl.program_id(tile_axis)
    for r0 in range(0, st, rows):
        pos = pos0 + tile * st + r0 + lax.broadcasted_iota(jnp.int32, (1, rows, G_C), 1)
        for g, wdw in enumerate(POOL_WINDOWS):
            cols = slice(g * G_C, (g + 1) * G_C)
            acc = ue_ref[:, pl.ds(POOL_HALO + r0, rows), cols]
            u = acc
            for k in range(1, wdw):
                acc = acc + ue_ref[:, pl.ds(POOL_HALO + r0 - k, rows), cols]
            cnt = jnp.minimum(pos + 1, wdw).astype(F32)
            d_ref[:, r0:r0 + rows, cols] = acc / cnt - u


def _pool_diff_seq(u3, ts):
    nseq, s, ch = u3.shape
    cur, prev = _halo_specs(ts, POOL_HALO, ch)
    return pl.pallas_call(
        _with_history(functools.partial(_pool_kernel, rows=64, tile_axis=1, pos0=0), POOL_HALO),
        grid=(nseq, s // ts),
        in_specs=[cur, prev],
        out_specs=cur,
        out_shape=jax.ShapeDtypeStruct((nseq, s, ch), F32),
        scratch_shapes=[pltpu.VMEM((1, ts + POOL_HALO, ch), F32)],
        compiler_params=_params(("parallel", "arbitrary")),
        name="pool_diff_seq",
    )(u3, u3)


def _pool_diff(ue_tiles, nb, st, pos0):
    n = ue_tiles.shape[0]
    ch = ue_tiles.shape[2]
    blk = lambda r: pl.BlockSpec((nb, r, ch), lambda i: (i, 0, 0))
    return pl.pallas_call(
        functools.partial(_pool_kernel, rows=min(st, 64), tile_axis=None, pos0=pos0),
        grid=(n // nb,),
        in_specs=[blk(st + POOL_HALO)],
        out_specs=blk(st),
        out_shape=jax.ShapeDtypeStruct((n, st, ch), F32),
        compiler_params=_params(("parallel",)),
        name="pool_diff",
    )(ue_tiles)


def _ffn_ln(x1, w1_ref, w3_ref, w2_ref, g2_ref, b2_ref):
    xb = x1.astype(BF16)
    h1 = _dot(xb, w1_ref[...])
    h3 = _dot(xb, w3_ref[...])
    gated = (h1 * _sigmoid(h1) * h3).astype(BF16)
    return _layer_norm(ALPHA * x1 + _dot(gated, w2_ref[...]), g2_ref[...], b2_ref[...])


def _post_a_kernel(x_ref, oa_ref, c_ref, wo1_ref, wo2_ref, g1_ref, b1_ref, w1_ref, w3_ref, w2_ref, g2_ref, b2_ref,
                   o_ref):
    y = _dot(oa_ref[...].astype(BF16), wo1_ref[...]) + _dot(c_ref[...].astype(BF16), wo2_ref[...])
    x1 = _layer_norm(ALPHA * x_ref[...] + y, g1_ref[...], b1_ref[...])
    o_ref[...] = _ffn_ln(x1, w1_ref, w3_ref, w2_ref, g2_ref, b2_ref)


def _post_c_kernel(x_ref, d_ref, wg_ref, sc_ref, wo_ref, g1_ref, b1_ref, w1_ref, w3_ref, w2_ref, g2_ref, b2_ref,
                   o_ref):
    db = d_ref[...].astype(BF16)
    z = jnp.concatenate([_dot(db[:, g * G_C:(g + 1) * G_C], wg_ref[g]) for g in range(len(POOL_WINDOWS))], axis=1)
    y = _dot((z * sc_ref[...]).astype(BF16), wo_ref[...])
    x1 = _layer_norm(ALPHA * x_ref[...] + y, g1_ref[...], b1_ref[...])
    o_ref[...] = _ffn_ln(x1, w1_ref, w3_ref, w2_ref, g2_ref, b2_ref)


def _post(body, x, acts, consts, tm):
    t = x.shape[0]
    row = lambda a: pl.BlockSpec((tm, a.shape[1]), lambda i: (i, 0))
    return pl.pallas_call(
        body,
        grid=(t // tm,),
        in_specs=[row(x)] + [row(a) for a in acts] + [_const_spec(c.shape) for c in consts],
        out_specs=row(x),
        out_shape=jax.ShapeDtypeStruct(x.shape, F32),
        compiler_params=_params(("parallel",)),
        name=body.__name__.strip("_"),
    )(x, *acts, *consts)


def _rope_tables(pos):
    half = HD // 2
    inv = ROPE_THETA ** (-jnp.arange(half, dtype=F32) / half)
    ang = pos.astype(F32)[:, None] * inv[None, :]
    cos, sin = jnp.cos(ang), jnp.sin(ang)
    return jnp.tile(cos, (1, 4)), jnp.tile(jnp.concatenate([-sin, sin], axis=1), (1, 2))


def _with_hist_rows(u3, hist, halo):
    b, _, c = u3.shape
    return jnp.concatenate([jnp.zeros((b, halo - hist.shape[1], c), u3.dtype), hist, u3], axis=1)


def _prep_weights(p):
    n_a = p["w_in_a"].shape[0]
    cuts = np.cumsum([0, Q_COLS, 4 * KV_COLS, 2 * KV_COLS, GATE_COLS, C_B, C_B])
    out = {"a": [], "c": [], "ffn": []}
    for ia in range(n_a):
        w_in = p["w_in_a"][ia]
        wq, wkv, wwin, wgl, wga, wgb = [w_in[:, cuts[k]:cuts[k + 1]].astype(BF16) for k in range(6)]
        wglt = jnp.pad(wgl.T, ((0, GATE_ROWS - GATE_COLS), (0, 0)))
        wglr = jnp.pad(wgl, ((0, 0), (0, 128 - GATE_COLS)))
        eye = jnp.eye(4, dtype=F32)
        wk, wv = p["w_cmp_k"][ia], p["w_cmp_v"][ia]
        blocks = jnp.stack([wk, wk, wv, wv], axis=1)
        w4 = jnp.einsum("pkde,kl->pkdle", blocks, eye).reshape(L_CMP, 4 * HD, 4 * HD).astype(BF16)
        pe4 = jnp.concatenate([p["pe_cmp_k"][ia]] * 2 + [p["pe_cmp_v"][ia]] * 2, axis=1)
        wo = p["w_out_a"][ia].astype(BF16)
        out["a"].append(dict(
            inproj_t=(wq, wkv, wwin, wga, wgb, wglt), inproj_r=(wq, wkv, wwin, wga, wgb, wglr), w4=w4, pe4=pe4,
            conv=(jnp.pad(p["conv_w"][ia], ((0, 1), (0, 0))), p["conv_b"][ia][None], p["conv_ln_g"][ia][None],
                  p["conv_ln_b"][ia][None]),
            wo1=wo[:Q_COLS], wo2=wo[Q_COLS:]))
    for ic in range(p["w_in_c"].shape[0]):
        out["c"].append(dict(w_in=p["w_in_c"][ic].astype(BF16), w_grp=p["w_grp_c"][ic].astype(BF16),
                             scale=p["scale_c"][ic][None], w_out=p["w_out_c"][ic].astype(BF16)))
    for l in range(DEPTH):
        out["ffn"].append((p["ln1_g"][l][None], p["ln1_b"][l][None], p["w1"][l].astype(BF16),
                           p["w3"][l].astype(BF16), p["w2"][l].astype(BF16), p["ln2_g"][l][None], p["ln2_b"][l][None]))
    return out


def _trunk(x3, pos0, wb, wts, cache_kv, page_table, state_win, state_conv, state_pool):
    b, s, _ = x3.shape
    t = b * s
    prompt = cache_kv is None
    tm = min(512, t)
    x = x3.reshape(t, D_MODEL)
    cos, sin = _rope_tables(pos0 + jnp.arange(s, dtype=jnp.int32))
    if not prompt:
        cos, sin = jnp.tile(cos, (tm // s, 1)), jnp.tile(sin, (tm // s, 1))
    cosc, sinc = _rope_tables(jnp.arange(N_CMP, dtype=jnp.int32) * L_CMP + (L_CMP - 1))
    ts = 512 if prompt else s
    nb = 1 if prompt else 8
    if not prompt:
        n_a, n_pool, page = cache_kv.shape[:3]
        cache_t = jnp.transpose(cache_kv, (0, 1, 3, 4, 5, 2)).reshape(n_a * n_pool, 4 * KV_COLS, page)
        sw_t = jnp.transpose(state_win, (0, 1, 3, 4, 5, 2)).reshape(n_a * b, 2 * KV_COLS, wb)
        lane_blk = jnp.arange(128)[:, None]
        key_blk = jnp.arange(page_table.shape[1] * page + 128)[None, :] // L_SEL
        expand = ((lane_blk % 2 == 0) & (lane_blk // 2 == key_blk)).astype(BF16)
        out_row = jnp.arange(8 * L_CMP)[:, None]
        position = jnp.arange(8 * L_CMP)[None, :]
        perm = ((out_row // 8 == position % L_CMP) & (out_row % 8 == position // L_CMP)).astype(BF16)
    kv_new, win_new, conv_new, pool_new = [], [], [], []
    for l in range(DEPTH):
        ln1_g, ln1_b, w1, w3, w2, ln2_g, ln2_b = wts["ffn"][l]
        if l % 2 == 0:
            a = wts["a"][l // 2]
            q, kv, win, u, gates = _inproj_a(x, a["inproj_t" if prompt else "inproj_r"], cos, sin, tm, prompt)
            kv_new.append(kv.reshape(b, s, 4, KVH, HD))
            if prompt:
                oa = _attn_prompt(q, kv, win, gates, a["pe4"], a["w4"], cosc, sinc, b, s)
                win_new.append(win.reshape(b, s, 2, KVH, HD)[:, s - wb:])
                hist = None
            else:
                ia = l // 2
                n_pool = cache_kv.shape[1]
                oa, wout = _attn_sample(page_table + ia * n_pool, ia * b, q.reshape(b, s, Q_COLS),
                                        kv.reshape(b, s, 4 * KV_COLS), win.reshape(b, s, 2 * KV_COLS),
                                        gates.reshape(b, s, 128), sw_t, cache_t, a["pe4"], a["w4"], cosc, sinc, expand,
                                        perm)
                oa = oa.reshape(t, Q_COLS)
                win_new.append(wout.reshape(b, 2, KVH, HD, wb).transpose(0, 4, 1, 2, 3))
                hist = state_conv[ia]
            u3 = u.reshape(b, s, C_B)
            if prompt:
                c = _conv_module_seq(u3, *a["conv"], ts).reshape(t, C_B)
            else:
                c = _conv_module(_with_hist_rows(u3, hist, CONV_HALO), *a["conv"], nb, s).reshape(t, C_B)
            keep = CONV_W - 1
            conv_new.append(u3[:, s - keep:] if hist is None else jnp.concatenate([hist, u3], axis=1)[:, -keep:])
            x = _post(_post_a_kernel, x, (oa, c), (a["wo1"], a["wo2"], ln1_g, ln1_b, w1, w3, w2, ln2_g, ln2_b), tm)
        else:
            cw = wts["c"][l // 2]
            u3 = _matmul(x, cw["w_in"], tm).reshape(b, s, D_MODEL)
            hist = None if prompt else state_pool[l // 2]
            if prompt:
                d = _pool_diff_seq(u3, ts).reshape(t, D_MODEL)
            else:
                d = _pool_diff(_with_hist_rows(u3, hist, POOL_HALO), nb, s, pos0).reshape(t, D_MODEL)
            pool_new.append(u3[:, s - POOL_HIST:] if hist is None
                            else jnp.concatenate([hist, u3], axis=1)[:, -POOL_HIST:])
            x = _post(_post_c_kernel, x, (d,),
                      (cw["w_grp"], cw["scale"], cw["w_out"], ln1_g, ln1_b, w1, w3, w2, ln2_g, ln2_b), tm)
    return x.reshape(b, s, D_MODEL), jnp.stack(kv_new), jnp.stack(win_new), jnp.stack(conv_new), jnp.stack(pool_new)


def kernel(x_prompt, x_sample, cache_kv, state_win, state_conv, state_pool, page_table, w_in_a, w_cmp_k, pe_cmp_k, w_cmp_v, pe_cmp_v, conv_w, conv_b, conv_ln_g, conv_ln_b, w_out_a, w_in_c, w_grp_c, scale_c, w_out_c, ln1_g, ln1_b, ln2_g, ln2_b, w1, w3, w2):
    p = dict(w_in_a=w_in_a, w_cmp_k=w_cmp_k, pe_cmp_k=pe_cmp_k, w_cmp_v=w_cmp_v, pe_cmp_v=pe_cmp_v,
             conv_w=conv_w, conv_b=conv_b, conv_ln_g=conv_ln_g, conv_ln_b=conv_ln_b, w_out_a=w_out_a,
             w_in_c=w_in_c, w_grp_c=w_grp_c, scale_c=scale_c, w_out_c=w_out_c,
             ln1_g=ln1_g, ln1_b=ln1_b, ln2_g=ln2_g, ln2_b=ln2_b, w1=w1, w3=w3, w2=w2)
    wts = _prep_weights(p)
    past_len = page_table.shape[1] * cache_kv.shape[2]
    wb = state_win.shape[2]
    y_p, kv_p, win_p, conv_p, pool_p = _trunk(x_prompt, 0, wb, wts, None, None, None, None, None)
    y_s, kv_s, win_s, conv_s, pool_s = _trunk(x_sample, past_len, wb, wts, cache_kv, page_table, state_win,
                                              state_conv, state_pool)
    return (y_p, y_s, kv_p, kv_s, win_p, win_s, conv_p, conv_s, pool_p, pool_s)
```

```python
import functools

import numpy as np
import jax
import jax.numpy as jnp
from jax import lax
from jax.experimental import pallas as pl
from jax.experimental.pallas import tpu as pltpu

F32 = jnp.float32
BF16 = jnp.bfloat16

D_MODEL = 1024
DEPTH = 4
HD = 64
H_A = 8
KVH = 2
HPG = H_A // KVH
L_CMP = 32
L_SEL = 64
N_SEL = 8
WINDOW = 512
ROPE_THETA = 10000.0
FORCE_BONUS = 1.0e4
NEG_INF = -1.0e30
TINY = 1.0e-30
C_B = D_MODEL // 2
CONV_W = 31
POOL_WINDOWS = (2, 4, 8, 16)
G_C = D_MODEL // len(POOL_WINDOWS)
POOL_HIST = max(POOL_WINDOWS) - 1
D_FF = 2816
ALPHA = (2 * DEPTH) ** 0.25
Q_COLS = H_A * HD
KV_COLS = KVH * HD
GATE_COLS = 3 * H_A
GATE_ROWS = 32
Q_BLOCK = 256
N_CMP = 64
SEL_CHUNK = 256
WIN_SPAN = WINDOW + Q_BLOCK
SAMPLE_SEQS = 2
CONV_HALO = 32
POOL_HALO = 16
VMEM_LIMIT = 56 * 1024 * 1024


def _dot(a, b):
    return jnp.dot(a, b, preferred_element_type=F32)


def _dot_nt(a, b):
    return lax.dot_general(a, b, (((1,), (1,)), ((), ())), preferred_element_type=F32)


def _dot_tn(a, b):
    return lax.dot_general(a, b, (((0,), (0,)), ((), ())), preferred_element_type=F32)


def _sigmoid(x):
    return 1.0 / (1.0 + jnp.exp(-x))


def _layer_norm(y, g, b):
    mu = jnp.mean(y, axis=-1, keepdims=True)
    d = y - mu
    var = jnp.mean(d * d, axis=-1, keepdims=True)
    return d * lax.rsqrt(var + 1e-5) * g + b


def _rope128(v, cos, sin_signed):
    lane = lax.broadcasted_iota(jnp.int32, v.shape, 1)
    rot = jnp.where((lane % HD) < HD // 2, pltpu.roll(v, 128 - HD // 2, 1), pltpu.roll(v, HD // 2, 1))
    return v * cos + rot * sin_signed


def _params(sem=None):
    return pltpu.CompilerParams(dimension_semantics=sem, vmem_limit_bytes=VMEM_LIMIT)


def _const_spec(shape):
    nd = len(shape)
    return pl.BlockSpec(shape, lambda *_: (0,) * nd, pipeline_mode=pl.Buffered(1))


def _inproj_a_kernel(x_ref, wq_ref, wkv_ref, wwin_ref, wga_ref, wgb_ref, wgl_ref, cos_ref, sin_ref,
                     q_out, kv_out, win_out, u_out, g_out, *, gates_transposed):
    xb = x_ref[...].astype(BF16)
    cos = cos_ref[...]
    sin = sin_ref[...]
    q = _dot(xb, wq_ref[...])
    for c in range(Q_COLS // 128):
        q_out[:, c * 128:(c + 1) * 128] = _rope128(q[:, c * 128:(c + 1) * 128], cos, sin) * (HD ** -0.5)
    kv = _dot(xb, wkv_ref[...])
    kv_out[:, 0:256] = kv[:, 0:256]
    kv_out[:, 256:384] = _rope128(kv[:, 256:384], cos, sin)
    kv_out[:, 384:512] = kv[:, 384:512]
    w = _dot(xb, wwin_ref[...])
    win_out[:, 0:128] = _rope128(w[:, 0:128], cos, sin)
    win_out[:, 128:256] = w[:, 128:256]
    u_out[...] = _dot(xb, wga_ref[...]) * _sigmoid(_dot(xb, wgb_ref[...]))
    g_out[...] = _sigmoid(_dot_nt(wgl_ref[...], xb) if gates_transposed else _dot(xb, wgl_ref[...]))


def _inproj_a(x, wts, cos, sin, tm, gates_transposed):
    t = x.shape[0]
    g_spec = pl.BlockSpec((GATE_ROWS, tm), lambda i: (0, i)) if gates_transposed else pl.BlockSpec((tm, 128), lambda i: (i, 0))
    g_shape = (GATE_ROWS, t) if gates_transposed else (t, 128)
    n_tab = cos.shape[0] // tm
    row = lambda w: pl.BlockSpec((tm, w), lambda i: (i, 0))
    tab = pl.BlockSpec((tm, 128), lambda i: (i % n_tab, 0))
    return pl.pallas_call(
        functools.partial(_inproj_a_kernel, gates_transposed=gates_transposed),
        grid=(t // tm,),
        in_specs=[row(D_MODEL)] + [_const_spec(w.shape) for w in wts] + [tab, tab],
        out_specs=[row(Q_COLS), row(4 * KV_COLS), row(2 * KV_COLS), row(C_B), g_spec],
        out_shape=[jax.ShapeDtypeStruct((t, Q_COLS), F32), jax.ShapeDtypeStruct((t, 4 * KV_COLS), F32),
                   jax.ShapeDtypeStruct((t, 2 * KV_COLS), F32), jax.ShapeDtypeStruct((t, C_B), F32),
                   jax.ShapeDtypeStruct(g_shape, F32)],
        compiler_params=_params(("parallel",)),
        name="inproj_a",
    )(x, *wts, cos, sin)


def _compress(rows_at, pe4_ref, w4_ref, cosc_ref, sinc_ref, n_blocks=N_CMP):
    acc = jnp.zeros((n_blocks, 4 * HD), F32)
    for p in range(L_CMP):
        xp = rows_at(p) + pe4_ref[p:p + 1, :]
        acc = acc + _dot(xp.astype(BF16), w4_ref[p])
    reps = n_blocks // N_CMP
    k = _rope128(acc[:, 0:2 * HD], jnp.concatenate([cosc_ref[...]] * reps, axis=0),
                 jnp.concatenate([sinc_ref[...]] * reps, axis=0))
    v = acc[:, 2 * HD:4 * HD]
    return ([k[:, g * HD:(g + 1) * HD].astype(BF16) for g in range(KVH)],
            [v[:, g * HD:(g + 1) * HD].astype(BF16) for g in range(KVH)])


def _softmax_t(s, msk):
    s = jnp.where(msk, s, NEG_INF)
    m = jnp.max(s, axis=0, keepdims=True)
    p = jnp.where(msk, jnp.exp(s - m), 0.0)
    return m, p, jnp.sum(p, axis=0, keepdims=True)


def _nsa_core_t(qf, gt, t_q, t0, kcmp, vcmp, ksel, vselt, kwin, vwint, imp_scr, bias_scr, n_sel_blocks):
    nq = Q_BLOCK
    w = HPG * nq
    groups = range(KVH)
    t_lane = jnp.concatenate([t_q] * HPG, axis=1)
    row_c = lax.broadcasted_iota(jnp.int32, (N_CMP, w), 0)
    row_s = lax.broadcasted_iota(jnp.int32, (SEL_CHUNK, w), 0)
    row_q = lax.broadcasted_iota(jnp.int32, (nq, w), 0)
    sel_per_chunk = SEL_CHUNK // L_SEL
    n_full = t0 // SEL_CHUNK
    win_lo = pl.multiple_of(jnp.maximum(t0 + nq - WIN_SPAN, 0), nq)
    qgs = [jnp.concatenate([qf[:, (g * HPG + h) * HD:(g * HPG + h + 1) * HD] for h in range(HPG)],
                           axis=0).astype(BF16) for g in groups]
    o_c = []
    for g in groups:
        m_c = (row_c * L_CMP + (L_CMP - 1)) <= t_lane
        _, e_c, l_c = _softmax_t(_dot_nt(kcmp[g], qgs[g]), m_c)
        p_c = e_c / jnp.maximum(l_c, TINY)
        o_c.append(_dot_tn(vcmp[g], p_c.astype(BF16)))
        imp = p_c[:, 0:nq]
        for h in range(1, HPG):
            imp = imp + p_c[:, h * nq:(h + 1) * nq]
        half = N_CMP // 2
        pair_sums = []
        for c in range(nq // 128):
            slot = g * (nq // 128) + c
            imp_scr[slot] = imp[:, c * 128:(c + 1) * 128]
            pair_sums.append(imp_scr[slot, pl.ds(0, half, stride=2), :] + imp_scr[slot, pl.ds(1, half, stride=2), :])
        imp2 = jnp.concatenate(pair_sums, axis=1)
        blk = lax.broadcasted_iota(jnp.int32, (n_sel_blocks, nq), 0)
        cur = t_q // L_SEL
        forced = (blk == 0) | (blk == cur) | (blk == cur - 1)
        valid = (blk * L_SEL) <= t_q
        score = jnp.where(valid, imp2 + jnp.where(forced, FORCE_BONUS, 0.0), -FORCE_BONUS)
        cnt = jnp.zeros((n_sel_blocks, nq), jnp.int32)
        for k in range(n_sel_blocks):
            rk = score[k:k + 1, :]
            beats = (rk > score) | ((rk == score) & (blk > k))
            cnt = cnt + beats.astype(jnp.int32)
        bias_scr[g] = jnp.concatenate([jnp.where(cnt < N_SEL, 0.0, NEG_INF)] * HPG, axis=1)

    def sel_chunk(j, carries, diagonal):
        off = pl.multiple_of(j * SEL_CHUNK, SEL_CHUNK)
        out = []
        for g in groups:
            m, l, acc = carries[g]
            s = _dot_nt(ksel[g, pl.ds(off, SEL_CHUNK), :], qgs[g])
            s = jnp.concatenate([s[r * L_SEL:(r + 1) * L_SEL] + bias_scr[g, pl.ds(j * sel_per_chunk + r, 1), :]
                                 for r in range(sel_per_chunk)], axis=0)
            if diagonal:
                s = jnp.where((off + row_s) <= t_lane, s, NEG_INF)
            m_j = jnp.max(s, axis=0, keepdims=True)
            p = jnp.exp(s - m_j)
            m_new = jnp.maximum(m, m_j)
            a_old = jnp.exp(m - m_new)
            a_new = jnp.exp(m_j - m_new)
            pv = _dot(vselt[g, :, pl.ds(off, SEL_CHUNK)], p.astype(BF16))
            out.append((m_new, a_old * l + a_new * jnp.sum(p, axis=0, keepdims=True), a_old * acc + a_new * pv))
        return tuple(out)

    init = tuple((jnp.full((1, w), NEG_INF, F32), jnp.zeros((1, w), F32), jnp.zeros((HD, w), F32)) for _ in groups)
    carries = lax.fori_loop(0, n_full, functools.partial(sel_chunk, diagonal=False), init)
    carries = sel_chunk(n_full, carries, diagonal=True)
    outs = []
    for g in groups:
        _, l_s, a_s = carries[g]
        o_s = a_s / l_s
        s_w = _dot_nt(kwin[g, pl.ds(win_lo, WIN_SPAN), :], qgs[g])
        pieces = []
        for k in range(WIN_SPAN // nq):
            kpos = win_lo + k * nq + row_q
            ok = kpos <= t_lane
            if k == 0:
                ok = ok & (kpos > t_lane - WINDOW)
            pieces.append(jnp.where(ok, s_w[k * nq:(k + 1) * nq], NEG_INF))
        s_w = jnp.concatenate(pieces, axis=0)
        p_w = jnp.exp(s_w - jnp.max(s_w, axis=0, keepdims=True))
        o_w = (_dot(vwint[g, :, pl.ds(win_lo, WIN_SPAN)], p_w.astype(BF16))
               / jnp.sum(p_w, axis=0, keepdims=True))
        for h in range(HPG):
            r = 3 * (g * HPG + h)
            sl = slice(h * nq, (h + 1) * nq)
            outs.append(gt[r:r + 1, :] * o_c[g][:, sl] + gt[r + 1:r + 2, :] * o_s[:, sl]
                        + gt[r + 2:r + 3, :] * o_w[:, sl])
    return outs


def _attn_prompt_kernel(q_ref, kv_ref, win_ref, gt_ref, pe4_ref, w4_ref, cosc_ref, sinc_ref, o_ref,
                        kcf, ksel, vselt, kwin, vwint, kcmp, vcmp, imp_scr, bias_scr, *, seq):
    i = pl.program_id(1)

    @pl.when(i == 0)
    def _():
        v_t = kv_ref[:, 3 * KV_COLS:4 * KV_COLS].T.astype(BF16)
        vw_t = win_ref[:, KV_COLS:2 * KV_COLS].T.astype(BF16)
        for g in range(KVH):
            ksel[g] = kv_ref[:, 2 * KV_COLS + g * HD:2 * KV_COLS + (g + 1) * HD].astype(BF16)
            kwin[g] = win_ref[:, g * HD:(g + 1) * HD].astype(BF16)
            vselt[g] = v_t[g * HD:(g + 1) * HD]
            vwint[g] = vw_t[g * HD:(g + 1) * HD]
        for c in range(2):
            kcf[c] = kv_ref[:, c * KV_COLS:(c + 1) * KV_COLS]
        rows_at = lambda p: jnp.concatenate([kcf[c, pl.ds(p, N_CMP, stride=L_CMP), :] for c in range(2)], axis=1)
        k_c, v_c = _compress(rows_at, pe4_ref, w4_ref, cosc_ref, sinc_ref)
        for g in range(KVH):
            kcmp[g] = k_c[g]
            vcmp[g] = v_c[g]

    t0 = i * Q_BLOCK
    t_q = t0 + lax.broadcasted_iota(jnp.int32, (1, Q_BLOCK), 1)
    outs = _nsa_core_t(q_ref[...], gt_ref[...], t_q, t0, kcmp, vcmp, ksel, vselt, kwin, vwint, imp_scr, bias_scr,
                       n_sel_blocks=seq // L_SEL)
    for pair in range(H_A // 2):
        o_ref[:, pair * 128:(pair + 1) * 128] = jnp.concatenate(outs[2 * pair:2 * pair + 2], axis=0).T


def _attn_prompt(q, kv, win, gt, pe4, w4, cosc, sinc, batch, seq):
    t = q.shape[0]
    nqb = seq // Q_BLOCK
    qrow = lambda w: pl.BlockSpec((Q_BLOCK, w), lambda b, i: (b * nqb + i, 0))
    srow = lambda w: pl.BlockSpec((seq, w), lambda b, i: (b, 0))
    return pl.pallas_call(
        functools.partial(_attn_prompt_kernel, seq=seq),
        grid=(batch, nqb),
        in_specs=[qrow(Q_COLS), srow(4 * KV_COLS), srow(2 * KV_COLS),
                  pl.BlockSpec((GATE_ROWS, Q_BLOCK), lambda b, i: (0, b * nqb + i)),
                  _const_spec(pe4.shape), _const_spec(w4.shape), _const_spec(cosc.shape), _const_spec(sinc.shape)],
        out_specs=qrow(Q_COLS),
        out_shape=jax.ShapeDtypeStruct((t, Q_COLS), F32),
        scratch_shapes=[pltpu.VMEM((2, seq, KV_COLS), F32)]
        + [pltpu.VMEM((KVH, seq, HD), BF16), pltpu.VMEM((KVH, HD, seq), BF16)] * 2
        + [pltpu.VMEM((KVH, N_CMP, HD), BF16)] * 2
        + [pltpu.VMEM((KVH * Q_BLOCK // 128, N_CMP, 128), F32), pltpu.VMEM((KVH, seq // L_SEL, HPG * Q_BLOCK), F32)],
        compiler_params=_params(("parallel", "arbitrary")),
        name="attn_prompt",
    )(q, kv, win, gt, pe4, w4, cosc, sinc)


def _softmax_rows(parts):
    masked = [jnp.where(k, s, NEG_INF) for s, k in parts]
    m = masked[0].max(axis=1, keepdims=True)
    for s in masked[1:]:
        m = jnp.maximum(m, s.max(axis=1, keepdims=True))
    es = [jnp.where(k, jnp.exp(s - m), 0.0) for s, (_, k) in zip(masked, parts)]
    l = es[0].sum(axis=1, keepdims=True)
    for e in es[1:]:
        l = l + e.sum(axis=1, keepdims=True)
    return es, jnp.maximum(l, TINY)


def _attn_sample_kernel(pt_ref, q_ref, kv_ref, win_ref, gate_ref, sw_ref, *rest, n_pages, page, past):
    del pt_ref
    n_seq = q_ref.shape[0]
    pe4_ref, w4_ref, cosc_ref, sinc_ref, expand_ref, perm_ref, o_ref, wout_ref = rest[n_seq * n_pages:]
    group = perm_ref.shape[0] // page
    moved = []
    for r in range(n_seq):
        pages = rest[r * n_pages:(r + 1) * n_pages]
        for jj in range(n_pages // group):
            x_t = jnp.concatenate([pages[group * jj + k][0, 0:2 * KV_COLS, :] for k in range(group)], axis=1)
            hi = x_t.astype(BF16)
            lo = (x_t - hi.astype(F32)).astype(BF16)
            moved.append(_dot_nt(perm_ref[...], hi) + _dot_nt(perm_ref[...], lo))
    rows_at = lambda p: jnp.concatenate([m[8 * p:8 * p + 8] for m in moved], axis=0)
    kcmp, vcmp = _compress(rows_at, pe4_ref, w4_ref, cosc_ref, sinc_ref, n_seq * N_CMP)
    for r in range(n_seq):
        one = lambda a: [x[r * N_CMP:(r + 1) * N_CMP] for x in a]
        _attn_sample_seq(r, rest[r * n_pages:(r + 1) * n_pages], one(kcmp), one(vcmp), q_ref, kv_ref, win_ref,
                         gate_ref, sw_ref, expand_ref, o_ref, wout_ref, past)


def _attn_sample_seq(r, pages, kcmp, vcmp, q_ref, kv_ref, win_ref, gate_ref, sw_ref, expand_ref, o_ref, wout_ref,
                     past):
    n_pages = len(pages)
    nq = q_ref.shape[1]
    wb = sw_ref.shape[2]
    new_pad = 16
    cached = lambda r0: jnp.concatenate([pages[j][0, r0:r0 + HD, :].astype(BF16) for j in range(n_pages)], axis=1)
    kt = [cached(2 * KV_COLS + g * HD) for g in range(KVH)]
    vt = [cached(3 * KV_COLS + g * HD) for g in range(KVH)]

    qf = q_ref[r]
    kvn = kv_ref[r]
    wn = win_ref[r]
    gates = gate_ref[r]
    wout_ref[r, :, 0:wb - nq] = sw_ref[r, :, nq:wb]
    wout_ref[r, :, wb - nq:wb] = wn.T

    groups = range(KVH)
    rows = H_A * nq
    grp = lambda a, g: a[g * HPG * nq:(g + 1) * HPG * nq]
    per_group = lambda f: jnp.concatenate([f(g) for g in groups], axis=0)
    zpad = jnp.zeros((new_pad - nq, HD), F32)
    new_rows = lambda a: jnp.concatenate([a, zpad], axis=0).astype(BF16)
    ns_real = past // L_SEL + 1
    iq = lambda n: lax.broadcasted_iota(jnp.int32, (rows, n), 0) % nq
    ik = lambda n: lax.broadcasted_iota(jnp.int32, (rows, n), 1)
    causal_new = (ik(new_pad) <= iq(new_pad)) & (ik(new_pad) < nq)
    qg = [jnp.concatenate([qf[:, (g * HPG + h) * HD:(g * HPG + h + 1) * HD] for h in range(HPG)],
                          axis=0).astype(BF16) for g in groups]
    m_c = (ik(N_CMP) * L_CMP + (L_CMP - 1)) <= past + iq(N_CMP)
    (e_c,), l_c = _softmax_rows([(per_group(lambda g: _dot_nt(qg[g], kcmp[g])), m_c)])
    p_c = e_c / l_c
    o_c = per_group(lambda g: _dot(grp(p_c, g).astype(BF16), vcmp[g]))
    imp = per_group(lambda g: sum(p_c[(g * HPG + h) * nq:(g * HPG + h + 1) * nq] for h in range(HPG)))
    imp = jnp.concatenate([imp, jnp.zeros((KVH * nq, 128 - N_CMP), F32)], axis=1)
    imp2 = imp + pltpu.roll(imp, 127, 1)
    lane = lax.broadcasted_iota(jnp.int32, (KVH * nq, 128), 1)
    blk = lane // 2
    t_sel = past + lax.broadcasted_iota(jnp.int32, (KVH * nq, 128), 0) % nq
    cur = t_sel // L_SEL
    forced = (blk == 0) | (blk == cur) | (blk == cur - 1)
    valid = (blk * L_SEL) <= t_sel
    score = jnp.where(valid, imp2 + jnp.where(forced, FORCE_BONUS, 0.0), -FORCE_BONUS)
    score = jnp.where(((lane & 1) == 0) & (blk < ns_real), score, -jnp.inf)
    cnt = jnp.zeros((KVH * nq, 128), jnp.int32)
    for k in range(ns_real):
        ck = score[:, 2 * k:2 * k + 1]
        beats = (ck > score) | ((ck == score) & (lane > 2 * k))
        cnt = cnt + beats.astype(jnp.int32)
    sel = (cnt < N_SEL).astype(BF16)
    sel_rows = per_group(lambda g: jnp.concatenate([sel[g * nq:(g + 1) * nq]] * HPG, axis=0))
    picked = _dot(sel_rows, expand_ref[...]) > 0.5
    s_p = per_group(lambda g: _dot(qg[g], kt[g]))
    s_n = per_group(lambda g: _dot_nt(qg[g], new_rows(kvn[:, 2 * KV_COLS + g * HD:2 * KV_COLS + (g + 1) * HD])))
    (e_p, e_n), l_s = _softmax_rows([(s_p, picked[:, 0:past]), (s_n, picked[:, past:past + new_pad] & causal_new)])
    o_s = per_group(lambda g: _dot_nt(grp(e_p, g).astype(BF16), vt[g]) + _dot(
        grp(e_n, g).astype(BF16), new_rows(kvn[:, 3 * KV_COLS + g * HD:3 * KV_COLS + (g + 1) * HD]))) / l_s
    s_wp = per_group(lambda g: _dot(qg[g], sw_ref[r, g * HD:(g + 1) * HD, :].astype(BF16)))
    s_wn = per_group(lambda g: _dot_nt(qg[g], new_rows(wn[:, g * HD:(g + 1) * HD])))
    (e_wp, e_wn), l_w = _softmax_rows([(s_wp, ik(wb) > iq(wb) + (wb - WINDOW)), (s_wn, causal_new)])
    o_w = per_group(lambda g: _dot_nt(
        grp(e_wp, g).astype(BF16), sw_ref[r, KV_COLS + g * HD:KV_COLS + (g + 1) * HD, :].astype(BF16)) + _dot(
        grp(e_wn, g).astype(BF16), new_rows(wn[:, KV_COLS + g * HD:KV_COLS + (g + 1) * HD]))) / l_w
    gate = lambda br: jnp.concatenate([gates[:, 3 * hd + br:3 * hd + br + 1] for hd in range(H_A)], axis=0)
    o = gate(0) * o_c + gate(1) * o_s + gate(2) * o_w
    for pair in range(H_A // 2):
        o_ref[r, :, 2 * pair * HD:(2 * pair + 2) * HD] = jnp.concatenate(
            [o[2 * pair * nq:(2 * pair + 1) * nq], o[(2 * pair + 1) * nq:(2 * pair + 2) * nq]], axis=1)


def _attn_sample(page_ids, win_off, q3, kv3, win3, gates3, state_win_t, cache_t, pe4, w4, cosc, sinc, expand, perm):
    nb, nq, _ = q3.shape
    n_pages = page_ids.shape[1]
    page = cache_t.shape[2]
    past = n_pages * page
    wb = state_win_t.shape[2]
    ns = SAMPLE_SEQS
    seq3 = lambda r, w: pl.BlockSpec((ns, r, w), lambda b, pt: (b, 0, 0))
    page_spec = lambda r, j: pl.BlockSpec((1, 4 * KV_COLS, page), lambda b, pt: (pt[ns * b + r, j], 0, 0))
    const = lambda a: pl.BlockSpec(a.shape, lambda b, pt: (0,) * a.ndim, pipeline_mode=pl.Buffered(1))
    grid_spec = pltpu.PrefetchScalarGridSpec(
        num_scalar_prefetch=1,
        grid=(nb // ns,),
        in_specs=[seq3(nq, Q_COLS), seq3(nq, 4 * KV_COLS), seq3(nq, 2 * KV_COLS), seq3(nq, 128),
                  pl.BlockSpec((ns, 2 * KV_COLS, wb), lambda b, pt: (win_off // ns + b, 0, 0))]
        + [page_spec(r, j) for r in range(ns) for j in range(n_pages)]
        + [const(pe4), const(w4), const(cosc), const(sinc), const(expand), const(perm)],
        out_specs=[seq3(nq, Q_COLS), seq3(2 * KV_COLS, wb)],
    )
    return pl.pallas_call(
        functools.partial(_attn_sample_kernel, n_pages=n_pages, page=page, past=past),
        grid_spec=grid_spec,
        out_shape=[jax.ShapeDtypeStruct((nb, nq, Q_COLS), F32), jax.ShapeDtypeStruct((nb, 2 * KV_COLS, wb), F32)],
        compiler_params=_params(("arbitrary",)),
        name="attn_sample",
    )(page_ids, q3, kv3, win3, gates3, state_win_t, *([cache_t] * (ns * n_pages)), pe4, w4, cosc, sinc, expand, perm)


def _conv_kernel(ue_ref, w_ref, b_ref, g_ref, bb_ref, c_ref, shift_scr, *, rows):
    nb, st, ch = c_ref.shape
    pad = CONV_HALO - (CONV_W - 1)
    for r0 in range(0, st, rows):
        acc = jnp.zeros((nb, rows, ch), F32)
        for res in range(8):
            taps = range(res, CONV_W, 8)
            span = 8 * (len(taps) - 1) + rows
            shift_scr[res, :, 0:span, :] = ue_ref[:, pl.ds(pad + r0 + res, span), :]
            for m, k in enumerate(taps):
                acc = acc + shift_scr[res, :, 8 * m:8 * m + rows, :] * w_ref[k:k + 1, :].reshape(1, 1, ch)
        y = _layer_norm(acc + b_ref[...].reshape(1, 1, ch), g_ref[...].reshape(1, 1, ch), bb_ref[...].reshape(1, 1, ch))
        c_ref[:, r0:r0 + rows, :] = y * _sigmoid(y)


def _with_history(body, halo):
    def kernel_fn(cur_ref, prev_ref, *rest):
        ue_scr = rest[-1]
        first = pl.program_id(1) == 0
        ue_scr[:, 0:halo, :] = jnp.where(first, 0.0, prev_ref[...])
        ue_scr[:, halo:, :] = cur_ref[...]
        body(ue_scr, *rest[:-1])
    return kernel_fn


def _conv_shift_scratch(nb, rows, ch):
    return pltpu.VMEM((8, nb, 8 * ((CONV_W - 1) // 8) + rows, ch), F32)


def _halo_specs(ts, halo, ch):
    cur = pl.BlockSpec((1, ts, ch), lambda b, i: (b, i, 0))
    prev = pl.BlockSpec((1, halo, ch), lambda b, i: (b, jnp.maximum(i * (ts // halo) - 1, 0), 0))
    return cur, prev


def _conv_module_seq(u3, w, b, g, bb, ts):
    nseq, s, ch = u3.shape
    cur, prev = _halo_specs(ts, CONV_HALO, ch)
    consts = (w, b, g, bb)
    return pl.pallas_call(
        _with_history(functools.partial(_conv_kernel, rows=32), CONV_HALO),
        grid=(nseq, s // ts),
        in_specs=[cur, prev] + [pl.BlockSpec(a.shape, lambda bi, i: (0, 0), pipeline_mode=pl.Buffered(1)) for a in consts],
        out_specs=cur,
        out_shape=jax.ShapeDtypeStruct((nseq, s, ch), F32),
        scratch_shapes=[_conv_shift_scratch(1, 32, ch), pltpu.VMEM((1, ts + CONV_HALO, ch), F32)],
        compiler_params=_params(("parallel", "arbitrary")),
        name="conv_module_seq",
    )(u3, u3, *consts)


def _conv_module(ue_tiles, w, b, g, bb, nb, st):
    n = ue_tiles.shape[0]
    ch = ue_tiles.shape[2]
    blk = lambda r: pl.BlockSpec((nb, r, ch), lambda i: (i, 0, 0))
    return pl.pallas_call(
        functools.partial(_conv_kernel, rows=min(st, 32)),
        grid=(n // nb,),
        in_specs=[blk(st + CONV_HALO)] + [_const_spec(a.shape) for a in (w, b, g, bb)],
        out_specs=blk(st),
        out_shape=jax.ShapeDtypeStruct((n, st, ch), F32),
        scratch_shapes=[_conv_shift_scratch(nb, min(st, 32), ch)],
        compiler_params=_params(("parallel",)),
        name="conv_module",
    )(ue_tiles, w, b, g, bb)


def _matmul_kernel(x_ref, w_ref, o_ref):
    o_ref[...] = _dot(x_ref[...].astype(BF16), w_ref[...])


def _matmul(x, w, tm):
    t, k = x.shape
    n = w.shape[1]
    return pl.pallas_call(
        _matmul_kernel,
        grid=(t // tm,),
        in_specs=[pl.BlockSpec((tm, k), lambda i: (i, 0)), _const_spec(w.shape)],
        out_specs=pl.BlockSpec((tm, n), lambda i: (i, 0)),
        out_shape=jax.ShapeDtypeStruct((t, n), F32),
        compiler_params=_params(("parallel",)),
        name="pool_inproj",
    )(x, w)


def _pool_kernel(ue_ref, d_ref, *, rows, tile_axis, pos0):
    nb, st, _ = d_ref.shape
    tile = 0 if tile_axis is None else pl.program_id(tile_axis)
    for r0 in range(0, st, rows):
        pos = pos0 + tile * st + r0 + lax.broadcasted_iota(jnp.int32, (1, rows, G_C), 1)
        for g, wdw in enumerate(POOL_WINDOWS):
            cols = slice(g * G_C, (g + 1) * G_C)
            acc = ue_ref[:, pl.ds(POOL_HALO + r0, rows), cols]
            u = acc
            for k in range(1, wdw):
                acc = acc + ue_ref[:, pl.ds(POOL_HALO + r0 - k, rows), cols]
            cnt = jnp.minimum(pos + 1, wdw).astype(F32)
            d_ref[:, r0:r0 + rows, cols] = acc / cnt - u


def _pool_diff_seq(u3, ts):
    nseq, s, ch = u3.shape
    cur, prev = _halo_specs(ts, POOL_HALO, ch)
    return pl.pallas_call(
        _with_history(functools.partial(_pool_kernel, rows=64, tile_axis=1, pos0=0), POOL_HALO),
        grid=(nseq, s // ts),
        in_specs=[cur, prev],
        out_specs=cur,
        out_shape=jax.ShapeDtypeStruct((nseq, s, ch), F32),
        scratch_shapes=[pltpu.VMEM((1, ts + POOL_HALO, ch), F32)],
        compiler_params=_params(("parallel", "arbitrary")),
        name="pool_diff_seq",
    )(u3, u3)


def _pool_diff(ue_tiles, nb, st, pos0):
    n = ue_tiles.shape[0]
    ch = ue_tiles.shape[2]
    blk = lambda r: pl.BlockSpec((nb, r, ch), lambda i: (i, 0, 0))
    return pl.pallas_call(
        functools.partial(_pool_kernel, rows=min(st, 64), tile_axis=None, pos0=pos0),
        grid=(n // nb,),
        in_specs=[blk(st + POOL_HALO)],
        out_specs=blk(st),
        out_shape=jax.ShapeDtypeStruct((n, st, ch), F32),
        compiler_params=_params(("parallel",)),
        name="pool_diff",
    )(ue_tiles)


def _ffn_ln(x1, w1_ref, w3_ref, w2_ref, g2_ref, b2_ref):
    xb = x1.astype(BF16)
    h1 = _dot(xb, w1_ref[...])
    h3 = _dot(xb, w3_ref[...])
    gated = (h1 * _sigmoid(h1) * h3).astype(BF16)
    return _layer_norm(ALPHA * x1 + _dot(gated, w2_ref[...]), g2_ref[...], b2_ref[...])


def _post_a_kernel(x_ref, oa_ref, c_ref, wo1_ref, wo2_ref, g1_ref, b1_ref, w1_ref, w3_ref, w2_ref, g2_ref, b2_ref,
                   o_ref):
    y = _dot(oa_ref[...].astype(BF16), wo1_ref[...]) + _dot(c_ref[...].astype(BF16), wo2_ref[...])
    x1 = _layer_norm(ALPHA * x_ref[...] + y, g1_ref[...], b1_ref[...])
    o_ref[...] = _ffn_ln(x1, w1_ref, w3_ref, w2_ref, g2_ref, b2_ref)


def _post_c_kernel(x_ref, d_ref, wg_ref, sc_ref, wo_ref, g1_ref, b1_ref, w1_ref, w3_ref, w2_ref, g2_ref, b2_ref,
                   o_ref):
    db = d_ref[...].astype(BF16)
    z = jnp.concatenate([_dot(db[:, g * G_C:(g + 1) * G_C], wg_ref[g]) for g in range(len(POOL_WINDOWS))], axis=1)
    y = _dot((z * sc_ref[...]).astype(BF16), wo_ref[...])
    x1 = _layer_norm(ALPHA * x_ref[...] + y, g1_ref[...], b1_ref[...])
    o_ref[...] = _ffn_ln(x1, w1_ref, w3_ref, w2_ref, g2_ref, b2_ref)


def _post(body, x, acts, consts, tm):
    t = x.shape[0]
    row = lambda a: pl.BlockSpec((tm, a.shape[1]), lambda i: (i, 0))
    return pl.pallas_call(
        body,
        grid=(t // tm,),
        in_specs=[row(x)] + [row(a) for a in acts] + [_const_spec(c.shape) for c in consts],
        out_specs=row(x),
        out_shape=jax.ShapeDtypeStruct(x.shape, F32),
        compiler_params=_params(("parallel",)),
        name=body.__name__.strip("_"),
    )(x, *acts, *consts)


def _rope_tables(pos):
    half = HD // 2
    inv = ROPE_THETA ** (-jnp.arange(half, dtype=F32) / half)
    ang = pos.astype(F32)[:, None] * inv[None, :]
    cos, sin = jnp.cos(ang), jnp.sin(ang)
    return jnp.tile(cos, (1, 4)), jnp.tile(jnp.concatenate([-sin, sin], axis=1), (1, 2))


def _with_hist_rows(u3, hist, halo):
    b, _, c = u3.shape
    return jnp.concatenate([jnp.zeros((b, halo - hist.shape[1], c), u3.dtype), hist, u3], axis=1)


def _prep_weights(p):
    n_a = p["w_in_a"].shape[0]
    cuts = np.cumsum([0, Q_COLS, 4 * KV_COLS, 2 * KV_COLS, GATE_COLS, C_B, C_B])
    out = {"a": [], "c": [], "ffn": []}
    for ia in range(n_a):
        w_in = p["w_in_a"][ia]
        wq, wkv, wwin, wgl, wga, wgb = [w_in[:, cuts[k]:cuts[k + 1]].astype(BF16) for k in range(6)]
        wglt = jnp.pad(wgl.T, ((0, GATE_ROWS - GATE_COLS), (0, 0)))
        wglr = jnp.pad(wgl, ((0, 0), (0, 128 - GATE_COLS)))
        eye = jnp.eye(4, dtype=F32)
        wk, wv = p["w_cmp_k"][ia], p["w_cmp_v"][ia]
        blocks = jnp.stack([wk, wk, wv, wv], axis=1)
        w4 = jnp.einsum("pkde,kl->pkdle", blocks, eye).reshape(L_CMP, 4 * HD, 4 * HD).astype(BF16)
        pe4 = jnp.concatenate([p["pe_cmp_k"][ia]] * 2 + [p["pe_cmp_v"][ia]] * 2, axis=1)
        wo = p["w_out_a"][ia].astype(BF16)
        out["a"].append(dict(
            inproj_t=(wq, wkv, wwin, wga, wgb, wglt), inproj_r=(wq, wkv, wwin, wga, wgb, wglr), w4=w4, pe4=pe4,
            conv=(jnp.pad(p["conv_w"][ia], ((0, 1), (0, 0))), p["conv_b"][ia][None], p["conv_ln_g"][ia][None],
                  p["conv_ln_b"][ia][None]),
            wo1=wo[:Q_COLS], wo2=wo[Q_COLS:]))
    for ic in range(p["w_in_c"].shape[0]):
        out["c"].append(dict(w_in=p["w_in_c"][ic].astype(BF16), w_grp=p["w_grp_c"][ic].astype(BF16),
                             scale=p["scale_c"][ic][None], w_out=p["w_out_c"][ic].astype(BF16)))
    for l in range(DEPTH):
        out["ffn"].append((p["ln1_g"][l][None], p["ln1_b"][l][None], p["w1"][l].astype(BF16),
                           p["w3"][l].astype(BF16), p["w2"][l].astype(BF16), p["ln2_g"][l][None], p["ln2_b"][l][None]))
    return out


def _trunk(x3, pos0, wb, wts, cache_kv, page_table, state_win, state_conv, state_pool):
    b, s, _ = x3.shape
    t = b * s
    prompt = cache_kv is None
    tm = min(512, t)
    x = x3.reshape(t, D_MODEL)
    cos, sin = _rope_tables(pos0 + jnp.arange(s, dtype=jnp.int32))
    if not prompt:
        cos, sin = jnp.tile(cos, (tm // s, 1)), jnp.tile(sin, (tm // s, 1))
    cosc, sinc = _rope_tables(jnp.arange(N_CMP, dtype=jnp.int32) * L_CMP + (L_CMP - 1))
    ts = 512 if prompt else s
    nb = 1 if prompt else 8
    if not prompt:
        n_a, n_pool, page = cache_kv.shape[:3]
        cache_t = jnp.transpose(cache_kv, (0, 1, 3, 4, 5, 2)).reshape(n_a * n_pool, 4 * KV_COLS, page)
        sw_t = jnp.transpose(state_win, (0, 1, 3, 4, 5, 2)).reshape(n_a * b, 2 * KV_COLS, wb)
        lane_blk = jnp.arange(128)[:, None]
        key_blk = jnp.arange(page_table.shape[1] * page + 128)[None, :] // L_SEL
        expand = ((lane_blk % 2 == 0) & (lane_blk // 2 == key_blk)).astype(BF16)
        out_row = jnp.arange(8 * L_CMP)[:, None]
        position = jnp.arange(8 * L_CMP)[None, :]
        perm = ((out_row // 8 == position % L_CMP) & (out_row % 8 == position // L_CMP)).astype(BF16)
    kv_new, win_new, conv_new, pool_new = [], [], [], []
    for l in range(DEPTH):
        ln1_g, ln1_b, w1, w3, w2, ln2_g, ln2_b = wts["ffn"][l]
        if l % 2 == 0:
            a = wts["a"][l // 2]
            q, kv, win, u, gates = _inproj_a(x, a["inproj_t" if prompt else "inproj_r"], cos, sin, tm, prompt)
            kv_new.append(kv.reshape(b, s, 4, KVH, HD))
            if prompt:
                oa = _attn_prompt(q, kv, win, gates, a["pe4"], a["w4"], cosc, sinc, b, s)
                win_new.append(win.reshape(b, s, 2, KVH, HD)[:, s - wb:])
                hist = None
            else:
                ia = l // 2
                n_pool = cache_kv.shape[1]
                oa, wout = _attn_sample(page_table + ia * n_pool, ia * b, q.reshape(b, s, Q_COLS),
                                        kv.reshape(b, s, 4 * KV_COLS), win.reshape(b, s, 2 * KV_COLS),
                                        gates.reshape(b, s, 128), sw_t, cache_t, a["pe4"], a["w4"], cosc, sinc, expand,
                                        perm)
                oa = oa.reshape(t, Q_COLS)
                win_new.append(wout.reshape(b, 2, KVH, HD, wb).transpose(0, 4, 1, 2, 3))
                hist = state_conv[ia]
            u3 = u.reshape(b, s, C_B)
            if prompt:
                c = _conv_module_seq(u3, *a["conv"], ts).reshape(t, C_B)
            else:
                c = _conv_module(_with_hist_rows(u3, hist, CONV_HALO), *a["conv"], nb, s).reshape(t, C_B)
            keep = CONV_W - 1
            conv_new.append(u3[:, s - keep:] if hist is None else jnp.concatenate([hist, u3], axis=1)[:, -keep:])
            x = _post(_post_a_kernel, x, (oa, c), (a["wo1"], a["wo2"], ln1_g, ln1_b, w1, w3, w2, ln2_g, ln2_b), tm)
        else:
            cw = wts["c"][l // 2]
            u3 = _matmul(x, cw["w_in"], tm).reshape(b, s, D_MODEL)
            hist = None if prompt else state_pool[l // 2]
            if prompt:
                d = _pool_diff_seq(u3, ts).reshape(t, D_MODEL)
            else:
                d = _pool_diff(_with_hist_rows(u3, hist, POOL_HALO), nb, s, pos0).reshape(t, D_MODEL)
            pool_new.append(u3[:, s - POOL_HIST:] if hist is None
                            else jnp.concatenate([hist, u3], axis=1)[:, -POOL_HIST:])
            x = _post(_post_c_kernel, x, (d,),
                      (cw["w_grp"], cw["scale"], cw["w_out"], ln1_g, ln1_b, w1, w3, w2, ln2_g, ln2_b), tm)
    return x.reshape(b, s, D_MODEL), jnp.stack(kv_new), jnp.stack(win_new), jnp.stack(conv_new), jnp.stack(pool_new)


def kernel(x_prompt, x_sample, cache_kv, state_win, state_conv, state_pool, page_table, w_in_a, w_cmp_k, pe_cmp_k, w_cmp_v, pe_cmp_v, conv_w, conv_b, conv_ln_g, conv_ln_b, w_out_a, w_in_c, w_grp_c, scale_c, w_out_c, ln1_g, ln1_b, ln2_g, ln2_b, w1, w3, w2):
    p = dict(w_in_a=w_in_a, w_cmp_k=w_cmp_k, pe_cmp_k=pe_cmp_k, w_cmp_v=w_cmp_v, pe_cmp_v=pe_cmp_v,
             conv_w=conv_w, conv_b=conv_b, conv_ln_g=conv_ln_g, conv_ln_b=conv_ln_b, w_out_a=w_out_a,
             w_in_c=w_in_c, w_grp_c=w_grp_c, scale_c=scale_c, w_out_c=w_out_c,
             ln1_g=ln1_g, ln1_b=ln1_b, ln2_g=ln2_g, ln2_b=ln2_b, w1=w1, w3=w3, w2=w2)
    wts = _prep_weights(p)
    past_len = page_table.shape[1] * cache_kv.shape[2]
    wb = state_win.shape[2]
    y_p, kv_p, win_p, conv_p, pool_p = _trunk(x_prompt, 0, wb, wts, None, None, None, None, None)
    y_s, kv_s, win_s, conv_s, pool_s = _trunk(x_sample, past_len, wb, wts, cache_kv, page_table, state_win,
                                              state_conv, state_pool)
    return (y_p, y_s, kv_p, kv_s, win_p, win_s, conv_p, conv_s, pool_p, pool_s)
```

```python
import functools

import numpy as np
import jax
import jax.numpy as jnp
from jax import lax
from jax.experimental import pallas as pl
from jax.experimental.pallas import tpu as pltpu

F32 = jnp.float32
BF16 = jnp.bfloat16

D_MODEL = 1024
DEPTH = 4
HD = 64
H_A = 8
KVH = 2
HPG = H_A // KVH
L_CMP = 32
L_SEL = 64
N_SEL = 8
WINDOW = 512
ROPE_THETA = 10000.0
FORCE_BONUS = 1.0e4
NEG_INF = -1.0e30
TINY = 1.0e-30
C_B = D_MODEL // 2
CONV_W = 31
POOL_WINDOWS = (2, 4, 8, 16)
G_C = D_MODEL // len(POOL_WINDOWS)
POOL_HIST = max(POOL_WINDOWS) - 1
D_FF = 2816
ALPHA = (2 * DEPTH) ** 0.25
Q_COLS = H_A * HD
KV_COLS = KVH * HD
GATE_COLS = 3 * H_A
GATE_ROWS = 32
Q_BLOCK = 256
N_CMP = 64
SEL_CHUNK = 256
WIN_SPAN = WINDOW + Q_BLOCK
SAMPLE_SEQS = 2
CONV_HALO = 32
POOL_HALO = 16
VMEM_LIMIT = 56 * 1024 * 1024
LOG2E = 1.4426950408889634
Q_SCALE = HD ** -0.5 * LOG2E


def _dot(a, b):
    return jnp.dot(a, b, preferred_element_type=F32)


def _dot_nt(a, b):
    return lax.dot_general(a, b, (((1,), (1,)), ((), ())), preferred_element_type=F32)


def _dot_tn(a, b):
    return lax.dot_general(a, b, (((0,), (0,)), ((), ())), preferred_element_type=F32)


def _sigmoid(x):
    return 1.0 / (1.0 + jnp.exp(-x))


def _layer_norm(y, g, b):
    mu = jnp.mean(y, axis=-1, keepdims=True)
    d = y - mu
    var = jnp.mean(d * d, axis=-1, keepdims=True)
    return d * lax.rsqrt(var + 1e-5) * g + b


def _rope128(v, cos, sin_signed):
    lane = lax.broadcasted_iota(jnp.int32, v.shape, 1)
    rot = jnp.where((lane % HD) < HD // 2, pltpu.roll(v, 128 - HD // 2, 1), pltpu.roll(v, HD // 2, 1))
    return v * cos + rot * sin_signed


def _params(sem=None):
    return pltpu.CompilerParams(dimension_semantics=sem, vmem_limit_bytes=VMEM_LIMIT)


def _const_spec(shape):
    nd = len(shape)
    return pl.BlockSpec(shape, lambda *_: (0,) * nd, pipeline_mode=pl.Buffered(1))


def _inproj_a_kernel(x_ref, wq_ref, wkv_ref, wwin_ref, wga_ref, wgb_ref, wgl_ref, cos_ref, sin_ref, *rest,
                     gates_transposed, n_prev):
    q_out, kv_all_out, win_out, u_out, g_out = rest[-5:]
    if n_prev:
        kv_all_out[0:n_prev] = rest[0][...]
    kv_out = kv_all_out.at[n_prev]
    xb = x_ref[...].astype(BF16)
    cos = cos_ref[...]
    sin = sin_ref[...]
    q = _dot(xb, wq_ref[...])
    for c in range(Q_COLS // 128):
        q_out[:, c * 128:(c + 1) * 128] = _rope128(q[:, c * 128:(c + 1) * 128], cos, sin) * Q_SCALE
    kv = _dot(xb, wkv_ref[...])
    kv_out[:, 0:256] = kv[:, 0:256]
    kv_out[:, 256:384] = _rope128(kv[:, 256:384], cos, sin)
    kv_out[:, 384:512] = kv[:, 384:512]
    w = _dot(xb, wwin_ref[...])
    win_out[:, 0:128] = _rope128(w[:, 0:128], cos, sin)
    win_out[:, 128:256] = w[:, 128:256]
    u_out[...] = _dot(xb, wga_ref[...]) * _sigmoid(_dot(xb, wgb_ref[...]))
    g_out[...] = _sigmoid(_dot_nt(wgl_ref[...], xb) if gates_transposed else _dot(xb, wgl_ref[...]))


def _inproj_a(x, wts, cos, sin, tm, gates_transposed, kv_prev):
    t = x.shape[0]
    n_prev = 0 if kv_prev is None else kv_prev.shape[0]
    kv_spec = lambda n: pl.BlockSpec((n, tm, 4 * KV_COLS), lambda i: (0, i, 0))
    g_spec = pl.BlockSpec((GATE_ROWS, tm), lambda i: (0, i)) if gates_transposed else pl.BlockSpec((tm, 128), lambda i: (i, 0))
    g_shape = (GATE_ROWS, t) if gates_transposed else (t, 128)
    n_tab = cos.shape[0] // tm
    row = lambda w: pl.BlockSpec((tm, w), lambda i: (i, 0))
    tab = pl.BlockSpec((tm, 128), lambda i: (i % n_tab, 0))
    return pl.pallas_call(
        functools.partial(_inproj_a_kernel, gates_transposed=gates_transposed, n_prev=n_prev),
        grid=(t // tm,),
        in_specs=[row(D_MODEL)] + [_const_spec(w.shape) for w in wts] + [tab, tab] + [kv_spec(n_prev)] * bool(n_prev),
        out_specs=[row(Q_COLS), kv_spec(n_prev + 1), row(2 * KV_COLS), row(C_B), g_spec],
        out_shape=[jax.ShapeDtypeStruct((t, Q_COLS), F32), jax.ShapeDtypeStruct((n_prev + 1, t, 4 * KV_COLS), F32),
                   jax.ShapeDtypeStruct((t, 2 * KV_COLS), F32), jax.ShapeDtypeStruct((t, C_B), F32),
                   jax.ShapeDtypeStruct(g_shape, F32)],
        compiler_params=_params(("parallel",)),
        name="inproj_a",
    )(x, *wts, cos, sin, *([kv_prev] * bool(n_prev)))


def _compress(rows_at, pe4_ref, w4_ref, cosc_ref, sinc_ref, n_blocks=N_CMP):
    acc = jnp.zeros((n_blocks, 4 * HD), F32)
    for p in range(L_CMP):
        xp = rows_at(p) + pe4_ref[p:p + 1, :]
        acc = acc + _dot(xp.astype(BF16), w4_ref[p])
    reps = n_blocks // N_CMP
    k = _rope128(acc[:, 0:2 * HD], jnp.concatenate([cosc_ref[...]] * reps, axis=0),
                 jnp.concatenate([sinc_ref[...]] * reps, axis=0))
    v = acc[:, 2 * HD:4 * HD]
    return ([k[:, g * HD:(g + 1) * HD].astype(BF16) for g in range(KVH)],
            [v[:, g * HD:(g + 1) * HD].astype(BF16) for g in range(KVH)])


def _softmax_t(s, msk):
    s = jnp.where(msk, s, NEG_INF)
    m = jnp.max(s, axis=0, keepdims=True)
    p = jnp.where(msk, jnp.exp2(s - m), 0.0)
    return m, p, jnp.sum(p, axis=0, keepdims=True)


def _nsa_core_t(qf, gt, t_q, t0, kcmp, vcmp, ksel, vselt, kwin, vwint, imp_scr, bias_scr, n_sel_blocks):
    nq = Q_BLOCK
    w = HPG * nq
    groups = range(KVH)
    t_lane = jnp.concatenate([t_q] * HPG, axis=1)
    row_c = lax.broadcasted_iota(jnp.int32, (N_CMP, w), 0)
    key_row = lax.broadcasted_iota(jnp.int32, (nq, nq), 0)
    heads = lambda a: jnp.concatenate([a] * HPG, axis=1)
    sel_per_chunk = SEL_CHUNK // L_SEL
    n_full = t0 // SEL_CHUNK
    win_lo = pl.multiple_of(jnp.maximum(t0 + nq - WIN_SPAN, 0), nq)
    causal_bias = heads(jnp.where(t0 + key_row <= t_q, 0.0, NEG_INF))
    win_bias = []
    for k in range(WIN_SPAN // nq):
        kpos = win_lo + k * nq + key_row
        ok = kpos <= t_q
        if k == 0:
            ok = ok & (kpos > t_q - WINDOW)
        win_bias.append(heads(jnp.where(ok, 0.0, NEG_INF)))
    qgs = [jnp.concatenate([qf[:, (g * HPG + h) * HD:(g * HPG + h + 1) * HD] for h in range(HPG)],
                           axis=0).astype(BF16) for g in groups]
    o_c = []
    for g in groups:
        m_c = (row_c * L_CMP + (L_CMP - 1)) <= t_lane
        _, e_c, l_c = _softmax_t(_dot_nt(kcmp[g], qgs[g]), m_c)
        p_c = e_c / jnp.maximum(l_c, TINY)
        o_c.append(_dot_tn(vcmp[g], p_c.astype(BF16)))
        imp = p_c[:, 0:nq]
        for h in range(1, HPG):
            imp = imp + p_c[:, h * nq:(h + 1) * nq]
        half = N_CMP // 2
        pair_sums = []
        for c in range(nq // 128):
            slot = g * (nq // 128) + c
            imp_scr[slot] = imp[:, c * 128:(c + 1) * 128]
            pair_sums.append(imp_scr[slot, pl.ds(0, half, stride=2), :] + imp_scr[slot, pl.ds(1, half, stride=2), :])
        imp2 = jnp.concatenate(pair_sums, axis=1)
        blk = lax.broadcasted_iota(jnp.int32, (n_sel_blocks, nq), 0)
        cur = t_q // L_SEL
        forced = (blk == 0) | (blk == cur) | (blk == cur - 1)
        valid = (blk * L_SEL) <= t_q
        score = jnp.where(valid, imp2 + jnp.where(forced, FORCE_BONUS, 0.0), -FORCE_BONUS)
        cnt = jnp.zeros((n_sel_blocks, nq), jnp.int32)
        for k in range(n_sel_blocks):
            rk = score[k:k + 1, :]
            beats = (rk > score) | ((rk == score) & (blk > k))
            cnt = cnt + beats.astype(jnp.int32)
        bias_scr[g] = jnp.concatenate([jnp.where(cnt < N_SEL, 0.0, NEG_INF)] * HPG, axis=1)

    def sel_chunk(j, carries, diagonal):
        off = pl.multiple_of(j * SEL_CHUNK, SEL_CHUNK)
        out = []
        for g in groups:
            m, l, acc = carries[g]
            s = _dot_nt(ksel[g, pl.ds(off, SEL_CHUNK), :], qgs[g])
            s = jnp.concatenate([s[r * L_SEL:(r + 1) * L_SEL] + bias_scr[g, pl.ds(j * sel_per_chunk + r, 1), :]
                                 for r in range(sel_per_chunk)], axis=0)
            if diagonal:
                s = s + causal_bias
            m_j = jnp.max(s, axis=0, keepdims=True)
            p = jnp.exp2(s - m_j)
            m_new = jnp.maximum(m, m_j)
            a_old = jnp.exp2(m - m_new)
            a_new = jnp.exp2(m_j - m_new)
            pv = _dot(vselt[g, :, pl.ds(off, SEL_CHUNK)], p.astype(BF16))
            out.append((m_new, a_old * l + a_new * jnp.sum(p, axis=0, keepdims=True), a_old * acc + a_new * pv))
        return tuple(out)

    init = tuple((jnp.full((1, w), NEG_INF, F32), jnp.zeros((1, w), F32), jnp.zeros((HD, w), F32)) for _ in groups)
    carries = lax.fori_loop(0, n_full, functools.partial(sel_chunk, diagonal=False), init)
    carries = sel_chunk(n_full, carries, diagonal=True)
    outs = []
    for g in groups:
        _, l_s, a_s = carries[g]
        o_s = a_s / l_s
        s_w = _dot_nt(kwin[g, pl.ds(win_lo, WIN_SPAN), :], qgs[g])
        s_w = jnp.concatenate([s_w[k * nq:(k + 1) * nq] + win_bias[k] for k in range(WIN_SPAN // nq)], axis=0)
        p_w = jnp.exp2(s_w - jnp.max(s_w, axis=0, keepdims=True))
        o_w = (_dot(vwint[g, :, pl.ds(win_lo, WIN_SPAN)], p_w.astype(BF16))
               / jnp.sum(p_w, axis=0, keepdims=True))
        for h in range(HPG):
            r = 3 * (g * HPG + h)
            sl = slice(h * nq, (h + 1) * nq)
            outs.append(gt[r:r + 1, :] * o_c[g][:, sl] + gt[r + 1:r + 2, :] * o_s[:, sl]
                        + gt[r + 2:r + 3, :] * o_w[:, sl])
    return outs


def _attn_prompt_kernel(q_ref, kv_ref, win_ref, gt_ref, pe4_ref, w4_ref, cosc_ref, sinc_ref, o_ref,
                        kcf, ksel, vselt, kwin, vwint, kcmp, vcmp, imp_scr, bias_scr, *, seq):
    i = pl.program_id(1)

    @pl.when(i == 0)
    def _():
        v_t = kv_ref[:, 3 * KV_COLS:4 * KV_COLS].T.astype(BF16)
        vw_t = win_ref[:, KV_COLS:2 * KV_COLS].T.astype(BF16)
        for g in range(KVH):
            ksel[g] = kv_ref[:, 2 * KV_COLS + g * HD:2 * KV_COLS + (g + 1) * HD].astype(BF16)
            kwin[g] = win_ref[:, g * HD:(g + 1) * HD].astype(BF16)
            vselt[g] = v_t[g * HD:(g + 1) * HD]
            vwint[g] = vw_t[g * HD:(g + 1) * HD]
        for c in range(2):
            kcf[c] = kv_ref[:, c * KV_COLS:(c + 1) * KV_COLS]
        rows_at = lambda p: jnp.concatenate([kcf[c, pl.ds(p, N_CMP, stride=L_CMP), :] for c in range(2)], axis=1)
        k_c, v_c = _compress(rows_at, pe4_ref, w4_ref, cosc_ref, sinc_ref)
        for g in range(KVH):
            kcmp[g] = k_c[g]
            vcmp[g] = v_c[g]

    t0 = i * Q_BLOCK
    t_q = t0 + lax.broadcasted_iota(jnp.int32, (1, Q_BLOCK), 1)
    outs = _nsa_core_t(q_ref[...], gt_ref[...], t_q, t0, kcmp, vcmp, ksel, vselt, kwin, vwint, imp_scr, bias_scr,
                       n_sel_blocks=seq // L_SEL)
    for pair in range(H_A // 2):
        o_ref[:, pair * 128:(pair + 1) * 128] = jnp.concatenate(outs[2 * pair:2 * pair + 2], axis=0).T


def _attn_prompt(q, kv_all, win, gt, pe4, w4, cosc, sinc, batch, seq):
    t = q.shape[0]
    layer = kv_all.shape[0] - 1
    nqb = seq // Q_BLOCK
    qrow = lambda w: pl.BlockSpec((Q_BLOCK, w), lambda b, i: (b * nqb + i, 0))
    srow = lambda w: pl.BlockSpec((seq, w), lambda b, i: (b, 0))
    kv_rows = pl.BlockSpec((None, seq, 4 * KV_COLS), lambda b, i: (layer, b, 0))
    return pl.pallas_call(
        functools.partial(_attn_prompt_kernel, seq=seq),
        grid=(batch, nqb),
        in_specs=[qrow(Q_COLS), kv_rows, srow(2 * KV_COLS),
                  pl.BlockSpec((GATE_ROWS, Q_BLOCK), lambda b, i: (0, b * nqb + i)),
                  _const_spec(pe4.shape), _const_spec(w4.shape), _const_spec(cosc.shape), _const_spec(sinc.shape)],
        out_specs=qrow(Q_COLS),
        out_shape=jax.ShapeDtypeStruct((t, Q_COLS), F32),
        scratch_shapes=[pltpu.VMEM((2, seq, KV_COLS), F32)]
        + [pltpu.VMEM((KVH, seq, HD), BF16), pltpu.VMEM((KVH, HD, seq), BF16)] * 2
        + [pltpu.VMEM((KVH, N_CMP, HD), BF16)] * 2
        + [pltpu.VMEM((KVH * Q_BLOCK // 128, N_CMP, 128), F32), pltpu.VMEM((KVH, seq // L_SEL, HPG * Q_BLOCK), F32)],
        compiler_params=_params(("parallel", "arbitrary")),
        name="attn_prompt",
    )(q, kv_all, win, gt, pe4, w4, cosc, sinc)


def _softmax_rows(parts):
    masked = [jnp.where(k, s, NEG_INF) for s, k in parts]
    m = masked[0].max(axis=1, keepdims=True)
    for s in masked[1:]:
        m = jnp.maximum(m, s.max(axis=1, keepdims=True))
    es = [jnp.where(k, jnp.exp2(s - m), 0.0) for s, (_, k) in zip(masked, parts)]
    l = es[0].sum(axis=1, keepdims=True)
    for e in es[1:]:
        l = l + e.sum(axis=1, keepdims=True)
    return es, jnp.maximum(l, TINY)


def _attn_sample_kernel(pt_ref, q_ref, kv_ref, win_ref, gate_ref, sw_ref, *rest, n_pages, page, past):
    del pt_ref
    n_seq = q_ref.shape[0]
    pe4_ref, w4_ref, cosc_ref, sinc_ref, expand_ref, perm_ref = rest[n_seq * n_pages:n_seq * n_pages + 6]
    o_ref, wout_all_ref = rest[-2:]
    n_prev = wout_all_ref.shape[0] - 1
    if n_prev:
        wout_all_ref[0:n_prev] = rest[-3][...]
    wout_ref = wout_all_ref.at[n_prev]
    group = perm_ref.shape[0] // page
    moved = []
    for r in range(n_seq):
        pages = rest[r * n_pages:(r + 1) * n_pages]
        for jj in range(n_pages // group):
            x_t = jnp.concatenate([pages[group * jj + k][0, 0:2 * KV_COLS, :] for k in range(group)], axis=1)
            hi = x_t.astype(BF16)
            lo = (x_t - hi.astype(F32)).astype(BF16)
            moved.append(_dot_nt(perm_ref[...], hi) + _dot_nt(perm_ref[...], lo))
    rows_at = lambda p: jnp.concatenate([m[8 * p:8 * p + 8] for m in moved], axis=0)
    kcmp, vcmp = _compress(rows_at, pe4_ref, w4_ref, cosc_ref, sinc_ref, n_seq * N_CMP)
    for r in range(n_seq):
        one = lambda a: [x[r * N_CMP:(r + 1) * N_CMP] for x in a]
        _attn_sample_seq(r, rest[r * n_pages:(r + 1) * n_pages], one(kcmp), one(vcmp), q_ref, kv_ref, win_ref,
                         gate_ref, sw_ref, expand_ref, o_ref, wout_ref, past)


def _attn_sample_seq(r, pages, kcmp, vcmp, q_ref, kv_ref, win_ref, gate_ref, sw_ref, expand_ref, o_ref, wout_ref,
                     past):
    n_pages = len(pages)
    nq = q_ref.shape[1]
    wb = sw_ref.shape[2]
    new_pad = 16
    cached = lambda r0: jnp.concatenate([pages[j][0, r0:r0 + HD, :].astype(BF16) for j in range(n_pages)], axis=1)
    kt = [cached(2 * KV_COLS + g * HD) for g in range(KVH)]
    vt = [cached(3 * KV_COLS + g * HD) for g in range(KVH)]

    qf = q_ref[r]
    kvn = kv_ref[r]
    wn = win_ref[r]
    gates = gate_ref[r]
    wout_ref[r, :, 0:wb - nq] = sw_ref[r, :, nq:wb]
    wout_ref[r, :, wb - nq:wb] = wn.T

    groups = range(KVH)
    rows = H_A * nq
    grp = lambda a, g: a[g * HPG * nq:(g + 1) * HPG * nq]
    per_group = lambda f: jnp.concatenate([f(g) for g in groups], axis=0)
    zpad = jnp.zeros((new_pad - nq, HD), F32)
    new_rows = lambda a: jnp.concatenate([a, zpad], axis=0).astype(BF16)
    ns_real = past // L_SEL + 1
    iq = lambda n: lax.broadcasted_iota(jnp.int32, (rows, n), 0) % nq
    ik = lambda n: lax.broadcasted_iota(jnp.int32, (rows, n), 1)
    causal_new = (ik(new_pad) <= iq(new_pad)) & (ik(new_pad) < nq)
    qg = [jnp.concatenate([qf[:, (g * HPG + h) * HD:(g * HPG + h + 1) * HD] for h in range(HPG)],
                          axis=0).astype(BF16) for g in groups]
    m_c = (ik(N_CMP) * L_CMP + (L_CMP - 1)) <= past + iq(N_CMP)
    (e_c,), l_c = _softmax_rows([(per_group(lambda g: _dot_nt(qg[g], kcmp[g])), m_c)])
    p_c = e_c / l_c
    o_c = per_group(lambda g: _dot(grp(p_c, g).astype(BF16), vcmp[g]))
    imp = per_group(lambda g: sum(p_c[(g * HPG + h) * nq:(g * HPG + h + 1) * nq] for h in range(HPG)))
    imp = jnp.concatenate([imp, jnp.zeros((KVH * nq, 128 - N_CMP), F32)], axis=1)
    imp2 = imp + pltpu.roll(imp, 127, 1)
    lane = lax.broadcasted_iota(jnp.int32, (KVH * nq, 128), 1)
    blk = lane // 2
    t_sel = past + lax.broadcasted_iota(jnp.int32, (KVH * nq, 128), 0) % nq
    cur = t_sel // L_SEL
    forced = (blk == 0) | (blk == cur) | (blk == cur - 1)
    valid = (blk * L_SEL) <= t_sel
    score = jnp.where(valid, imp2 + jnp.where(forced, FORCE_BONUS, 0.0), -FORCE_BONUS)
    score = jnp.where(((lane & 1) == 0) & (blk < ns_real), score, -jnp.inf)
    cnt = jnp.zeros((KVH * nq, 128), jnp.int32)
    for k in range(ns_real):
        ck = score[:, 2 * k:2 * k + 1]
        beats = (ck > score) | ((ck == score) & (lane > 2 * k))
        cnt = cnt + beats.astype(jnp.int32)
    sel = (cnt < N_SEL).astype(BF16)
    sel_rows = per_group(lambda g: jnp.concatenate([sel[g * nq:(g + 1) * nq]] * HPG, axis=0))
    picked = _dot(sel_rows, expand_ref[...]) > 0.5
    s_p = per_group(lambda g: _dot(qg[g], kt[g]))
    s_n = per_group(lambda g: _dot_nt(qg[g], new_rows(kvn[:, 2 * KV_COLS + g * HD:2 * KV_COLS + (g + 1) * HD])))
    (e_p, e_n), l_s = _softmax_rows([(s_p, picked[:, 0:past]), (s_n, picked[:, past:past + new_pad] & causal_new)])
    o_s = per_group(lambda g: _dot_nt(grp(e_p, g).astype(BF16), vt[g]) + _dot(
        grp(e_n, g).astype(BF16), new_rows(kvn[:, 3 * KV_COLS + g * HD:3 * KV_COLS + (g + 1) * HD]))) / l_s
    s_wp = per_group(lambda g: _dot(qg[g], sw_ref[r, g * HD:(g + 1) * HD, :].astype(BF16)))
    s_wn = per_group(lambda g: _dot_nt(qg[g], new_rows(wn[:, g * HD:(g + 1) * HD])))
    (e_wp, e_wn), l_w = _softmax_rows([(s_wp, ik(wb) > iq(wb) + (wb - WINDOW)), (s_wn, causal_new)])
    o_w = per_group(lambda g: _dot_nt(
        grp(e_wp, g).astype(BF16), sw_ref[r, KV_COLS + g * HD:KV_COLS + (g + 1) * HD, :].astype(BF16)) + _dot(
        grp(e_wn, g).astype(BF16), new_rows(wn[:, KV_COLS + g * HD:KV_COLS + (g + 1) * HD]))) / l_w
    gate = lambda br: jnp.concatenate([gates[:, 3 * hd + br:3 * hd + br + 1] for hd in range(H_A)], axis=0)
    o = gate(0) * o_c + gate(1) * o_s + gate(2) * o_w
    for pair in range(H_A // 2):
        o_ref[r, :, 2 * pair * HD:(2 * pair + 2) * HD] = jnp.concatenate(
            [o[2 * pair * nq:(2 * pair + 1) * nq], o[(2 * pair + 1) * nq:(2 * pair + 2) * nq]], axis=1)


def _attn_sample(page_ids, win_off, q3, kv3, win3, gates3, state_win_t, cache_t, pe4, w4, cosc, sinc, expand, perm,
                 wout_prev):
    nb, nq, _ = q3.shape
    n_prev = 0 if wout_prev is None else wout_prev.shape[0]
    n_pages = page_ids.shape[1]
    page = cache_t.shape[2]
    past = n_pages * page
    wb = state_win_t.shape[2]
    ns = SAMPLE_SEQS
    wout_spec = lambda n: pl.BlockSpec((n, ns, 2 * KV_COLS, wb), lambda b, pt: (0, b, 0, 0))
    seq3 = lambda r, w: pl.BlockSpec((ns, r, w), lambda b, pt: (b, 0, 0))
    page_spec = lambda r, j: pl.BlockSpec((1, 4 * KV_COLS, page), lambda b, pt: (pt[ns * b + r, j], 0, 0))
    const = lambda a: pl.BlockSpec(a.shape, lambda b, pt: (0,) * a.ndim, pipeline_mode=pl.Buffered(1))
    grid_spec = pltpu.PrefetchScalarGridSpec(
        num_scalar_prefetch=1,
        grid=(nb // ns,),
        in_specs=[seq3(nq, Q_COLS), seq3(nq, 4 * KV_COLS), seq3(nq, 2 * KV_COLS), seq3(nq, 128),
                  pl.BlockSpec((ns, 2 * KV_COLS, wb), lambda b, pt: (win_off // ns + b, 0, 0))]
        + [page_spec(r, j) for r in range(ns) for j in range(n_pages)]
        + [const(pe4), const(w4), const(cosc), const(sinc), const(expand), const(perm)]
        + [wout_spec(n_prev)] * bool(n_prev),
        out_specs=[seq3(nq, Q_COLS), wout_spec(n_prev + 1)],
    )
    return pl.pallas_call(
        functools.partial(_attn_sample_kernel, n_pages=n_pages, page=page, past=past),
        grid_spec=grid_spec,
        out_shape=[jax.ShapeDtypeStruct((nb, nq, Q_COLS), F32),
                   jax.ShapeDtypeStruct((n_prev + 1, nb, 2 * KV_COLS, wb), F32)],
        compiler_params=_params(("arbitrary",)),
        name="attn_sample",
    )(page_ids, q3, kv3, win3, gates3, state_win_t, *([cache_t] * (ns * n_pages)), pe4, w4, cosc, sinc, expand, perm,
      *([wout_prev] * bool(n_prev)))


def _conv_kernel(ue_ref, w_ref, b_ref, g_ref, bb_ref, c_ref, shift_scr, *, rows):
    nb, st, ch = c_ref.shape
    pad = CONV_HALO - (CONV_W - 1)
    for r0 in range(0, st, rows):
        acc = jnp.zeros((nb, rows, ch), F32)
        for res in range(8):
            taps = range(res, CONV_W, 8)
            span = 8 * (len(taps) - 1) + rows
            shift_scr[res, :, 0:span, :] = ue_ref[:, pl.ds(pad + r0 + res, span), :]
            for m, k in enumerate(taps):
                acc = acc + shift_scr[res, :, 8 * m:8 * m + rows, :] * w_ref[k:k + 1, :].reshape(1, 1, ch)
        y = _layer_norm(acc + b_ref[...].reshape(1, 1, ch), g_ref[...].reshape(1, 1, ch), bb_ref[...].reshape(1, 1, ch))
        c_ref[:, r0:r0 + rows, :] = y * _sigmoid(y)


def _with_history(body, halo):
    def kernel_fn(cur_ref, prev_ref, *rest):
        ue_scr = rest[-1]
        first = pl.program_id(1) == 0
        ue_scr[:, 0:halo, :] = jnp.where(first, 0.0, prev_ref[...])
        ue_scr[:, halo:, :] = cur_ref[...]
        body(ue_scr, *rest[:-1])
    return kernel_fn


def _conv_shift_scratch(nb, rows, ch):
    return pltpu.VMEM((8, nb, 8 * ((CONV_W - 1) // 8) + rows, ch), F32)


def _halo_specs(ts, halo, ch):
    cur = pl.BlockSpec((1, ts, ch), lambda b, i: (b, i, 0))
    prev = pl.BlockSpec((1, halo, ch), lambda b, i: (b, jnp.maximum(i * (ts // halo) - 1, 0), 0))
    return cur, prev


def _conv_module_seq(u3, w, b, g, bb, ts):
    nseq, s, ch = u3.shape
    cur, prev = _halo_specs(ts, CONV_HALO, ch)
    consts = (w, b, g, bb)
    return pl.pallas_call(
        _with_history(functools.partial(_conv_kernel, rows=32), CONV_HALO),
        grid=(nseq, s // ts),
        in_specs=[cur, prev] + [pl.BlockSpec(a.shape, lambda bi, i: (0, 0), pipeline_mode=pl.Buffered(1)) for a in consts],
        out_specs=cur,
        out_shape=jax.ShapeDtypeStruct((nseq, s, ch), F32),
        scratch_shapes=[_conv_shift_scratch(1, 32, ch), pltpu.VMEM((1, ts + CONV_HALO, ch), F32)],
        compiler_params=_params(("parallel", "arbitrary")),
        name="conv_module_seq",
    )(u3, u3, *consts)


def _conv_module(ue_tiles, w, b, g, bb, nb, st):
    n = ue_tiles.shape[0]
    ch = ue_tiles.shape[2]
    blk = lambda r: pl.BlockSpec((nb, r, ch), lambda i: (i, 0, 0))
    return pl.pallas_call(
        functools.partial(_conv_kernel, rows=min(st, 32)),
        grid=(n // nb,),
        in_specs=[blk(st + CONV_HALO)] + [_const_spec(a.shape) for a in (w, b, g, bb)],
        out_specs=blk(st),
        out_shape=jax.ShapeDtypeStruct((n, st, ch), F32),
        scratch_shapes=[_conv_shift_scratch(nb, min(st, 32), ch)],
        compiler_params=_params(("parallel",)),
        name="conv_module",
    )(ue_tiles, w, b, g, bb)


def _matmul_kernel(x_ref, w_ref, o_ref):
    o_ref[...] = _dot(x_ref[...].astype(BF16), w_ref[...])


def _matmul(x, w, tm):
    t, k = x.shape
    n = w.shape[1]
    return pl.pallas_call(
        _matmul_kernel,
        grid=(t // tm,),
        in_specs=[pl.BlockSpec((tm, k), lambda i: (i, 0)), _const_spec(w.shape)],
        out_specs=pl.BlockSpec((tm, n), lambda i: (i, 0)),
        out_shape=jax.ShapeDtypeStruct((t, n), F32),
        compiler_params=_params(("parallel",)),
        name="pool_inproj",
    )(x, w)


def _pool_kernel(ue_ref, d_ref, *, rows, tile_axis, pos0):
    nb, st, _ = d_ref.shape
    tile = 0 if tile_axis is None else pl.program_id(tile_axis)
    for r0 in range(0, st, rows):
        pos = pos0 + tile * st + r0 + lax.broadcasted_iota(jnp.int32, (1, rows, G_C), 1)
        for g, wdw in enumerate(POOL_WINDOWS):
            cols = slice(g * G_C, (g + 1) * G_C)
            acc = ue_ref[:, pl.ds(POOL_HALO + r0, rows), cols]
            u = acc
            for k in range(1, wdw):
                acc = acc + ue_ref[:, pl.ds(POOL_HALO + r0 - k, rows), cols]
            cnt = jnp.minimum(pos + 1, wdw).astype(F32)
            d_ref[:, r0:r0 + rows, cols] = acc / cnt - u


def _pool_diff_seq(u3, ts):
    nseq, s, ch = u3.shape
    cur, prev = _halo_specs(ts, POOL_HALO, ch)
    return pl.pallas_call(
        _with_history(functools.partial(_pool_kernel, rows=64, tile_axis=1, pos0=0), POOL_HALO),
        grid=(nseq, s // ts),
        in_specs=[cur, prev],
        out_specs=cur,
        out_shape=jax.ShapeDtypeStruct((nseq, s, ch), F32),
        scratch_shapes=[pltpu.VMEM((1, ts + POOL_HALO, ch), F32)],
        compiler_params=_params(("parallel", "arbitrary")),
        name="pool_diff_seq",
    )(u3, u3)


def _pool_diff(ue_tiles, nb, st, pos0):
    n = ue_tiles.shape[0]
    ch = ue_tiles.shape[2]
    blk = lambda r: pl.BlockSpec((nb, r, ch), lambda i: (i, 0, 0))
    return pl.pallas_call(
        functools.partial(_pool_kernel, rows=min(st, 64), tile_axis=None, pos0=pos0),
        grid=(n // nb,),
        in_specs=[blk(st + POOL_HALO)],
        out_specs=blk(st),
        out_shape=jax.ShapeDtypeStruct((n, st, ch), F32),
        compiler_params=_params(("parallel",)),
        name="pool_diff",
    )(ue_tiles)


def _ffn_ln(x1, w1_ref, w3_ref, w2_ref, g2_ref, b2_ref):
    xb = x1.astype(BF16)
    h1 = _dot(xb, w1_ref[...])
    h3 = _dot(xb, w3_ref[...])
    gated = (h1 * _sigmoid(h1) * h3).astype(BF16)
    return _layer_norm(ALPHA * x1 + _dot(gated, w2_ref[...]), g2_ref[...], b2_ref[...])


def _post_a_kernel(x_ref, oa_ref, c_ref, wo1_ref, wo2_ref, g1_ref, b1_ref, w1_ref, w3_ref, w2_ref, g2_ref, b2_ref,
                   o_ref):
    y = _dot(oa_ref[...].astype(BF16), wo1_ref[...]) + _dot(c_ref[...].astype(BF16), wo2_ref[...])
    x1 = _layer_norm(ALPHA * x_ref[...] + y, g1_ref[...], b1_ref[...])
    o_ref[...] = _ffn_ln(x1, w1_ref, w3_ref, w2_ref, g2_ref, b2_ref)


def _post_c_kernel(x_ref, d_ref, wg_ref, sc_ref, wo_ref, g1_ref, b1_ref, w1_ref, w3_ref, w2_ref, g2_ref, b2_ref,
                   o_ref):
    db = d_ref[...].astype(BF16)
    z = jnp.concatenate([_dot(db[:, g * G_C:(g + 1) * G_C], wg_ref[g]) for g in range(len(POOL_WINDOWS))], axis=1)
    y = _dot((z * sc_ref[...]).astype(BF16), wo_ref[...])
    x1 = _layer_norm(ALPHA * x_ref[...] + y, g1_ref[...], b1_ref[...])
    o_ref[...] = _ffn_ln(x1, w1_ref, w3_ref, w2_ref, g2_ref, b2_ref)


def _post(body, x, acts, consts, tm):
    t = x.shape[0]
    row = lambda a: pl.BlockSpec((tm, a.shape[1]), lambda i: (i, 0))
    return pl.pallas_call(
        body,
        grid=(t // tm,),
        in_specs=[row(x)] + [row(a) for a in acts] + [_const_spec(c.shape) for c in consts],
        out_specs=row(x),
        out_shape=jax.ShapeDtypeStruct(x.shape, F32),
        compiler_params=_params(("parallel",)),
        name=body.__name__.strip("_"),
    )(x, *acts, *consts)


def _rope_tables(pos):
    half = HD // 2
    inv = ROPE_THETA ** (-jnp.arange(half, dtype=F32) / half)
    ang = pos.astype(F32)[:, None] * inv[None, :]
    cos, sin = jnp.cos(ang), jnp.sin(ang)
    return jnp.tile(cos, (1, 4)), jnp.tile(jnp.concatenate([-sin, sin], axis=1), (1, 2))


def _with_hist_rows(u3, hist, halo):
    b, _, c = u3.shape
    return jnp.concatenate([jnp.zeros((b, halo - hist.shape[1], c), u3.dtype), hist, u3], axis=1)


def _prep_weights(p):
    n_a = p["w_in_a"].shape[0]
    cuts = np.cumsum([0, Q_COLS, 4 * KV_COLS, 2 * KV_COLS, GATE_COLS, C_B, C_B])
    out = {"a": [], "c": [], "ffn": []}
    for ia in range(n_a):
        w_in = p["w_in_a"][ia]
        wq, wkv, wwin, wgl, wga, wgb = [w_in[:, cuts[k]:cuts[k + 1]].astype(BF16) for k in range(6)]
        wglt = jnp.pad(wgl.T, ((0, GATE_ROWS - GATE_COLS), (0, 0)))
        wglr = jnp.pad(wgl, ((0, 0), (0, 128 - GATE_COLS)))
        eye = jnp.eye(4, dtype=F32)
        wk, wv = p["w_cmp_k"][ia], p["w_cmp_v"][ia]
        blocks = jnp.stack([wk, wk, wv, wv], axis=1)
        w4 = jnp.einsum("pkde,kl->pkdle", blocks, eye).reshape(L_CMP, 4 * HD, 4 * HD).astype(BF16)
        pe4 = jnp.concatenate([p["pe_cmp_k"][ia]] * 2 + [p["pe_cmp_v"][ia]] * 2, axis=1)
        wo = p["w_out_a"][ia].astype(BF16)
        out["a"].append(dict(
            inproj_t=(wq, wkv, wwin, wga, wgb, wglt), inproj_r=(wq, wkv, wwin, wga, wgb, wglr), w4=w4, pe4=pe4,
            conv=(jnp.pad(p["conv_w"][ia], ((0, 1), (0, 0))), p["conv_b"][ia][None], p["conv_ln_g"][ia][None],
                  p["conv_ln_b"][ia][None]),
            wo1=wo[:Q_COLS], wo2=wo[Q_COLS:]))
    for ic in range(p["w_in_c"].shape[0]):
        out["c"].append(dict(w_in=p["w_in_c"][ic].astype(BF16), w_grp=p["w_grp_c"][ic].astype(BF16),
                             scale=p["scale_c"][ic][None], w_out=p["w_out_c"][ic].astype(BF16)))
    for l in range(DEPTH):
        out["ffn"].append((p["ln1_g"][l][None], p["ln1_b"][l][None], p["w1"][l].astype(BF16),
                           p["w3"][l].astype(BF16), p["w2"][l].astype(BF16), p["ln2_g"][l][None], p["ln2_b"][l][None]))
    return out


def _trunk(x3, pos0, wb, wts, cache_kv, page_table, state_win, state_conv, state_pool):
    b, s, _ = x3.shape
    t = b * s
    prompt = cache_kv is None
    tm = min(512, t)
    x = x3.reshape(t, D_MODEL)
    cos, sin = _rope_tables(pos0 + jnp.arange(s, dtype=jnp.int32))
    if not prompt:
        cos, sin = jnp.tile(cos, (tm // s, 1)), jnp.tile(sin, (tm // s, 1))
    cosc, sinc = _rope_tables(jnp.arange(N_CMP, dtype=jnp.int32) * L_CMP + (L_CMP - 1))
    ts = 512 if prompt else s
    nb = 1 if prompt else 8
    if not prompt:
        n_a, n_pool, page = cache_kv.shape[:3]
        cache_t = jnp.transpose(cache_kv, (0, 1, 3, 4, 5, 2)).reshape(n_a * n_pool, 4 * KV_COLS, page)
        sw_t = jnp.transpose(state_win, (0, 1, 3, 4, 5, 2)).reshape(n_a * b, 2 * KV_COLS, wb)
        lane_blk = jnp.arange(128)[:, None]
        key_blk = jnp.arange(page_table.shape[1] * page + 128)[None, :] // L_SEL
        expand = ((lane_blk % 2 == 0) & (lane_blk // 2 == key_blk)).astype(BF16)
        out_row = jnp.arange(8 * L_CMP)[:, None]
        position = jnp.arange(8 * L_CMP)[None, :]
        perm = ((out_row // 8 == position % L_CMP) & (out_row % 8 == position // L_CMP)).astype(BF16)
    kv_all, wout_all, win_new, conv_new, pool_new = None, None, [], [], []
    for l in range(DEPTH):
        ln1_g, ln1_b, w1, w3, w2, ln2_g, ln2_b = wts["ffn"][l]
        if l % 2 == 0:
            a = wts["a"][l // 2]
            q, kv_all, win, u, gates = _inproj_a(x, a["inproj_t" if prompt else "inproj_r"], cos, sin, tm, prompt,
                                                 kv_all)
            if prompt:
                oa = _attn_prompt(q, kv_all, win, gates, a["pe4"], a["w4"], cosc, sinc, b, s)
                win_new.append(win.reshape(b, s, 2, KVH, HD)[:, s - wb:])
                hist = None
            else:
                ia = l // 2
                n_pool = cache_kv.shape[1]
                oa, wout_all = _attn_sample(page_table + ia * n_pool, ia * b, q.reshape(b, s, Q_COLS),
                                            kv_all[ia].reshape(b, s, 4 * KV_COLS), win.reshape(b, s, 2 * KV_COLS),
                                            gates.reshape(b, s, 128), sw_t, cache_t, a["pe4"], a["w4"], cosc, sinc,
                                            expand, perm, wout_all)
                oa = oa.reshape(t, Q_COLS)
                hist = state_conv[ia]
            u3 = u.reshape(b, s, C_B)
            keep = CONV_W - 1
            conv_new.append(u3[:, s - keep:] if hist is None else jnp.concatenate([hist, u3], axis=1)[:, -keep:])
            if prompt:
                c = _conv_module_seq(u3, *a["conv"], ts).reshape(t, C_B)
            else:
                c = _conv_module(_with_hist_rows(u3, hist, CONV_HALO), *a["conv"], nb, s).reshape(t, C_B)
            x = _post(_post_a_kernel, x, (oa, c), (a["wo1"], a["wo2"], ln1_g, ln1_b, w1, w3, w2, ln2_g, ln2_b), tm)
        else:
            cw = wts["c"][l // 2]
            u3 = _matmul(x, cw["w_in"], tm).reshape(b, s, D_MODEL)
            hist = None if prompt else state_pool[l // 2]
            if prompt:
                d = _pool_diff_seq(u3, ts).reshape(t, D_MODEL)
            else:
                d = _pool_diff(_with_hist_rows(u3, hist, POOL_HALO), nb, s, pos0).reshape(t, D_MODEL)
            pool_new.append(u3[:, s - POOL_HIST:] if hist is None
                            else jnp.concatenate([hist, u3], axis=1)[:, -POOL_HIST:])
            x = _post(_post_c_kernel, x, (d,),
                      (cw["w_grp"], cw["scale"], cw["w_out"], ln1_g, ln1_b, w1, w3, w2, ln2_g, ln2_b), tm)
    kv_out = kv_all.reshape(-1, b, s, 4, KVH, HD)
    if prompt:
        win_out = jnp.stack(win_new)
    else:
        win_out = wout_all.reshape(-1, b, 2, KVH, HD, wb).transpose(0, 1, 5, 2, 3, 4)
    return x.reshape(b, s, D_MODEL), kv_out, win_out, jnp.stack(conv_new), jnp.stack(pool_new)


def kernel(x_prompt, x_sample, cache_kv, state_win, state_conv, state_pool, page_table, w_in_a, w_cmp_k, pe_cmp_k, w_cmp_v, pe_cmp_v, conv_w, conv_b, conv_ln_g, conv_ln_b, w_out_a, w_in_c, w_grp_c, scale_c, w_out_c, ln1_g, ln1_b, ln2_g, ln2_b, w1, w3, w2):
    p = dict(w_in_a=w_in_a, w_cmp_k=w_cmp_k, pe_cmp_k=pe_cmp_k, w_cmp_v=w_cmp_v, pe_cmp_v=pe_cmp_v,
             conv_w=conv_w, conv_b=conv_b, conv_ln_g=conv_ln_g, conv_ln_b=conv_ln_b, w_out_a=w_out_a,
             w_in_c=w_in_c, w_grp_c=w_grp_c, scale_c=scale_c, w_out_c=w_out_c,
             ln1_g=ln1_g, ln1_b=ln1_b, ln2_g=ln2_g, ln2_b=ln2_b, w1=w1, w3=w3, w2=w2)
    wts = _prep_weights(p)
    past_len = page_table.shape[1] * cache_kv.shape[2]
    wb = state_win.shape[2]
    y_p, kv_p, win_p, conv_p, pool_p = _trunk(x_prompt, 0, wb, wts, None, None, None, None, None)
    y_s, kv_s, win_s, conv_s, pool_s = _trunk(x_sample, past_len, wb, wts, cache_kv, page_table, state_win,
                                              state_conv, state_pool)
    return (y_p, y_s, kv_p, kv_s, win_p, win_s, conv_p, conv_s, pool_p, pool_s)
```

```python
import functools

import numpy as np
import jax
import jax.numpy as jnp
from jax import lax
from jax.experimental import pallas as pl
from jax.experimental.pallas import tpu as pltpu

F32 = jnp.float32
BF16 = jnp.bfloat16

D_MODEL = 1024
DEPTH = 4
HD = 64
H_A = 8
KVH = 2
HPG = H_A // KVH
L_CMP = 32
L_SEL = 64
N_SEL = 8
WINDOW = 512
ROPE_THETA = 10000.0
FORCE_BONUS = 1.0e4
NEG_INF = -1.0e30
TINY = 1.0e-30
C_B = D_MODEL // 2
CONV_W = 31
POOL_WINDOWS = (2, 4, 8, 16)
G_C = D_MODEL // len(POOL_WINDOWS)
POOL_HIST = max(POOL_WINDOWS) - 1
D_FF = 2816
ALPHA = (2 * DEPTH) ** 0.25
Q_COLS = H_A * HD
KV_COLS = KVH * HD
GATE_COLS = 3 * H_A
GATE_ROWS = 32
Q_BLOCK = 256
N_CMP = 64
SEL_CHUNK = 256
WIN_SPAN = WINDOW + Q_BLOCK
SAMPLE_SEQS = 2
CONV_HALO = 32
POOL_HALO = 16
VMEM_LIMIT = 56 * 1024 * 1024
LOG2E = 1.4426950408889634
Q_SCALE = HD ** -0.5 * LOG2E


def _dot(a, b):
    return jnp.dot(a, b, preferred_element_type=F32)


def _dot_nt(a, b):
    return lax.dot_general(a, b, (((1,), (1,)), ((), ())), preferred_element_type=F32)


def _dot_tn(a, b):
    return lax.dot_general(a, b, (((0,), (0,)), ((), ())), preferred_element_type=F32)


def _sigmoid(x):
    return 1.0 / (1.0 + jnp.exp(-x))


def _layer_norm(y, g, b):
    mu = jnp.mean(y, axis=-1, keepdims=True)
    d = y - mu
    var = jnp.mean(d * d, axis=-1, keepdims=True)
    return d * lax.rsqrt(var + 1e-5) * g + b


def _rope128(v, cos, sin_signed):
    lane = lax.broadcasted_iota(jnp.int32, v.shape, 1)
    rot = jnp.where((lane % HD) < HD // 2, pltpu.roll(v, 128 - HD // 2, 1), pltpu.roll(v, HD // 2, 1))
    return v * cos + rot * sin_signed


def _params(sem=None):
    return pltpu.CompilerParams(dimension_semantics=sem, vmem_limit_bytes=VMEM_LIMIT)


def _const_spec(shape):
    nd = len(shape)
    return pl.BlockSpec(shape, lambda *_: (0,) * nd, pipeline_mode=pl.Buffered(1))


def _inproj_a_kernel(x_ref, wq_ref, wkv_ref, wwin_ref, wga_ref, wgb_ref, wgl_ref, cos_ref, sin_ref, *rest,
                     gates_transposed, n_prev):
    q_out, kv_out, win_out, u_out, g_out, kvt_all_out = rest[-6:]
    if n_prev:
        kvt_all_out[0:n_prev] = rest[0][...]
    xb = x_ref[...].astype(BF16)
    cos = cos_ref[...]
    sin = sin_ref[...]
    q = _dot(xb, wq_ref[...])
    for c in range(Q_COLS // 128):
        q_out[:, c * 128:(c + 1) * 128] = _rope128(q[:, c * 128:(c + 1) * 128], cos, sin) * Q_SCALE
    kv = _dot(xb, wkv_ref[...])
    kv_out[:, 0:256] = kv[:, 0:256]
    kv_out[:, 256:384] = _rope128(kv[:, 256:384], cos, sin)
    kv_out[:, 384:512] = kv[:, 384:512]
    kvt_all_out[n_prev, 0] = kv_out[...].T
    w = _dot(xb, wwin_ref[...])
    win_out[:, 0:128] = _rope128(w[:, 0:128], cos, sin)
    win_out[:, 128:256] = w[:, 128:256]
    u_out[...] = _dot(xb, wga_ref[...]) * _sigmoid(_dot(xb, wgb_ref[...]))
    g_out[...] = _sigmoid(_dot_nt(wgl_ref[...], xb) if gates_transposed else _dot(xb, wgl_ref[...]))


def _inproj_a(x, wts, cos, sin, tm, gates_transposed, kvt_prev, seq):
    t = x.shape[0]
    n_prev = 0 if kvt_prev is None else kvt_prev.shape[0]
    nt = max(seq // tm, 1)
    kvt_spec = lambda n: pl.BlockSpec((n, 1, 4 * KV_COLS, tm), lambda i: (0, i // nt, 0, i % nt))
    kvt_cols = max(seq, tm)
    g_spec = pl.BlockSpec((GATE_ROWS, tm), lambda i: (0, i)) if gates_transposed else pl.BlockSpec((tm, 128), lambda i: (i, 0))
    g_shape = (GATE_ROWS, t) if gates_transposed else (t, 128)
    n_tab = cos.shape[0] // tm
    row = lambda w: pl.BlockSpec((tm, w), lambda i: (i, 0))
    tab = pl.BlockSpec((tm, 128), lambda i: (i % n_tab, 0))
    return pl.pallas_call(
        functools.partial(_inproj_a_kernel, gates_transposed=gates_transposed, n_prev=n_prev),
        grid=(t // tm,),
        in_specs=[row(D_MODEL)] + [_const_spec(w.shape) for w in wts] + [tab, tab] + [kvt_spec(n_prev)] * bool(n_prev),
        out_specs=[row(Q_COLS), row(4 * KV_COLS), row(2 * KV_COLS), row(C_B), g_spec, kvt_spec(n_prev + 1)],
        out_shape=[jax.ShapeDtypeStruct((t, Q_COLS), F32), jax.ShapeDtypeStruct((t, 4 * KV_COLS), F32),
                   jax.ShapeDtypeStruct((t, 2 * KV_COLS), F32), jax.ShapeDtypeStruct((t, C_B), F32),
                   jax.ShapeDtypeStruct(g_shape, F32),
                   jax.ShapeDtypeStruct((n_prev + 1, t // kvt_cols, 4 * KV_COLS, kvt_cols), F32)],
        compiler_params=_params(("parallel",)),
        name="inproj_a",
    )(x, *wts, cos, sin, *([kvt_prev] * bool(n_prev)))


def _compress(rows_at, pe4_ref, w4_ref, cosc_ref, sinc_ref, n_blocks=N_CMP):
    acc = jnp.zeros((n_blocks, 4 * HD), F32)
    for p in range(L_CMP):
        xp = rows_at(p) + pe4_ref[p:p + 1, :]
        acc = acc + _dot(xp.astype(BF16), w4_ref[p])
    reps = n_blocks // N_CMP
    k = _rope128(acc[:, 0:2 * HD], jnp.concatenate([cosc_ref[...]] * reps, axis=0),
                 jnp.concatenate([sinc_ref[...]] * reps, axis=0))
    v = acc[:, 2 * HD:4 * HD]
    return ([k[:, g * HD:(g + 1) * HD].astype(BF16) for g in range(KVH)],
            [v[:, g * HD:(g + 1) * HD].astype(BF16) for g in range(KVH)])


def _softmax_t(s, msk):
    s = jnp.where(msk, s, NEG_INF)
    m = jnp.max(s, axis=0, keepdims=True)
    p = jnp.where(msk, jnp.exp2(s - m), 0.0)
    return m, p, jnp.sum(p, axis=0, keepdims=True)


def _nsa_core_t(qf, gt, t_q, t0, kcmp, vcmp, ksel, vselt, kwin, vwint, imp_scr, bias_scr, n_sel_blocks):
    nq = Q_BLOCK
    w = HPG * nq
    groups = range(KVH)
    t_lane = jnp.concatenate([t_q] * HPG, axis=1)
    row_c = lax.broadcasted_iota(jnp.int32, (N_CMP, w), 0)
    key_row = lax.broadcasted_iota(jnp.int32, (nq, nq), 0)
    heads = lambda a: jnp.concatenate([a] * HPG, axis=1)
    sel_per_chunk = SEL_CHUNK // L_SEL
    n_full = t0 // SEL_CHUNK
    win_lo = pl.multiple_of(jnp.maximum(t0 + nq - WIN_SPAN, 0), nq)
    causal_bias = heads(jnp.where(t0 + key_row <= t_q, 0.0, NEG_INF))
    win_bias = []
    for k in range(WIN_SPAN // nq):
        kpos = win_lo + k * nq + key_row
        ok = kpos <= t_q
        if k == 0:
            ok = ok & (kpos > t_q - WINDOW)
        win_bias.append(heads(jnp.where(ok, 0.0, NEG_INF)))
    qgs = [jnp.concatenate([qf[:, (g * HPG + h) * HD:(g * HPG + h + 1) * HD] for h in range(HPG)],
                           axis=0).astype(BF16) for g in groups]
    o_c = []
    for g in groups:
        m_c = (row_c * L_CMP + (L_CMP - 1)) <= t_lane
        _, e_c, l_c = _softmax_t(_dot_nt(kcmp[g], qgs[g]), m_c)
        p_c = e_c / jnp.maximum(l_c, TINY)
        o_c.append(_dot_tn(vcmp[g], p_c.astype(BF16)))
        imp = p_c[:, 0:nq]
        for h in range(1, HPG):
            imp = imp + p_c[:, h * nq:(h + 1) * nq]
        half = N_CMP // 2
        pair_sums = []
        for c in range(nq // 128):
            slot = g * (nq // 128) + c
            imp_scr[slot] = imp[:, c * 128:(c + 1) * 128]
            pair_sums.append(imp_scr[slot, pl.ds(0, half, stride=2), :] + imp_scr[slot, pl.ds(1, half, stride=2), :])
        imp2 = jnp.concatenate(pair_sums, axis=1)
        blk = lax.broadcasted_iota(jnp.int32, (n_sel_blocks, nq), 0)
        cur = t_q // L_SEL
        forced = (blk == 0) | (blk == cur) | (blk == cur - 1)
        valid = (blk * L_SEL) <= t_q
        score = jnp.where(valid, imp2 + jnp.where(forced, FORCE_BONUS, 0.0), -FORCE_BONUS)
        cnt = jnp.zeros((n_sel_blocks, nq), jnp.int32)
        for k in range(n_sel_blocks):
            rk = score[k:k + 1, :]
            beats = (rk > score) | ((rk == score) & (blk > k))
            cnt = cnt + beats.astype(jnp.int32)
        bias_scr[g] = jnp.concatenate([jnp.where(cnt < N_SEL, 0.0, NEG_INF)] * HPG, axis=1)

    def sel_chunk(j, carries, diagonal):
        off = pl.multiple_of(j * SEL_CHUNK, SEL_CHUNK)
        out = []
        for g in groups:
            m, l, acc = carries[g]
            s = _dot_nt(ksel[g, pl.ds(off, SEL_CHUNK), :], qgs[g])
            s = jnp.concatenate([s[r * L_SEL:(r + 1) * L_SEL] + bias_scr[g, pl.ds(j * sel_per_chunk + r, 1), :]
                                 for r in range(sel_per_chunk)], axis=0)
            if diagonal:
                s = s + causal_bias
            m_j = jnp.max(s, axis=0, keepdims=True)
            p = jnp.exp2(s - m_j)
            m_new = jnp.maximum(m, m_j)
            a_old = jnp.exp2(m - m_new)
            a_new = jnp.exp2(m_j - m_new)
            pv = _dot(vselt[g, :, pl.ds(off, SEL_CHUNK)], p.astype(BF16))
            out.append((m_new, a_old * l + a_new * jnp.sum(p, axis=0, keepdims=True), a_old * acc + a_new * pv))
        return tuple(out)

    init = tuple((jnp.full((1, w), NEG_INF, F32), jnp.zeros((1, w), F32), jnp.zeros((HD, w), F32)) for _ in groups)
    carries = lax.fori_loop(0, n_full, functools.partial(sel_chunk, diagonal=False), init)
    carries = sel_chunk(n_full, carries, diagonal=True)
    outs = []
    for g in groups:
        _, l_s, a_s = carries[g]
        o_s = a_s / l_s
        s_w = _dot_nt(kwin[g, pl.ds(win_lo, WIN_SPAN), :], qgs[g])
        s_w = jnp.concatenate([s_w[k * nq:(k + 1) * nq] + win_bias[k] for k in range(WIN_SPAN // nq)], axis=0)
        p_w = jnp.exp2(s_w - jnp.max(s_w, axis=0, keepdims=True))
        o_w = (_dot(vwint[g, :, pl.ds(win_lo, WIN_SPAN)], p_w.astype(BF16))
               / jnp.sum(p_w, axis=0, keepdims=True))
        for h in range(HPG):
            r = 3 * (g * HPG + h)
            sl = slice(h * nq, (h + 1) * nq)
            outs.append(gt[r:r + 1, :] * o_c[g][:, sl] + gt[r + 1:r + 2, :] * o_s[:, sl]
                        + gt[r + 2:r + 3, :] * o_w[:, sl])
    return outs


def _attn_prompt_kernel(q_ref, kv_ref, win_ref, gt_ref, pe4_ref, w4_ref, cosc_ref, sinc_ref, o_ref,
                        kcf, ksel, vselt, kwin, vwint, kcmp, vcmp, imp_scr, bias_scr, *, seq):
    i = pl.program_id(1)

    @pl.when(i == 0)
    def _():
        v_t = kv_ref[:, 3 * KV_COLS:4 * KV_COLS].T.astype(BF16)
        vw_t = win_ref[:, KV_COLS:2 * KV_COLS].T.astype(BF16)
        for g in range(KVH):
            ksel[g] = kv_ref[:, 2 * KV_COLS + g * HD:2 * KV_COLS + (g + 1) * HD].astype(BF16)
            kwin[g] = win_ref[:, g * HD:(g + 1) * HD].astype(BF16)
            vselt[g] = v_t[g * HD:(g + 1) * HD]
            vwint[g] = vw_t[g * HD:(g + 1) * HD]
        for c in range(2):
            kcf[c] = kv_ref[:, c * KV_COLS:(c + 1) * KV_COLS]
        rows_at = lambda p: jnp.concatenate([kcf[c, pl.ds(p, N_CMP, stride=L_CMP), :] for c in range(2)], axis=1)
        k_c, v_c = _compress(rows_at, pe4_ref, w4_ref, cosc_ref, sinc_ref)
        for g in range(KVH):
            kcmp[g] = k_c[g]
            vcmp[g] = v_c[g]

    t0 = i * Q_BLOCK
    t_q = t0 + lax.broadcasted_iota(jnp.int32, (1, Q_BLOCK), 1)
    outs = _nsa_core_t(q_ref[...], gt_ref[...], t_q, t0, kcmp, vcmp, ksel, vselt, kwin, vwint, imp_scr, bias_scr,
                       n_sel_blocks=seq // L_SEL)
    for pair in range(H_A // 2):
        o_ref[:, pair * 128:(pair + 1) * 128] = jnp.concatenate(outs[2 * pair:2 * pair + 2], axis=0).T


def _attn_prompt(q, kv, win, gt, pe4, w4, cosc, sinc, batch, seq):
    t = q.shape[0]
    nqb = seq // Q_BLOCK
    qrow = lambda w: pl.BlockSpec((Q_BLOCK, w), lambda b, i: (b * nqb + i, 0))
    srow = lambda w: pl.BlockSpec((seq, w), lambda b, i: (b, 0))
    return pl.pallas_call(
        functools.partial(_attn_prompt_kernel, seq=seq),
        grid=(batch, nqb),
        in_specs=[qrow(Q_COLS), srow(4 * KV_COLS), srow(2 * KV_COLS),
                  pl.BlockSpec((GATE_ROWS, Q_BLOCK), lambda b, i: (0, b * nqb + i)),
                  _const_spec(pe4.shape), _const_spec(w4.shape), _const_spec(cosc.shape), _const_spec(sinc.shape)],
        out_specs=qrow(Q_COLS),
        out_shape=jax.ShapeDtypeStruct((t, Q_COLS), F32),
        scratch_shapes=[pltpu.VMEM((2, seq, KV_COLS), F32)]
        + [pltpu.VMEM((KVH, seq, HD), BF16), pltpu.VMEM((KVH, HD, seq), BF16)] * 2
        + [pltpu.VMEM((KVH, N_CMP, HD), BF16)] * 2
        + [pltpu.VMEM((KVH * Q_BLOCK // 128, N_CMP, 128), F32), pltpu.VMEM((KVH, seq // L_SEL, HPG * Q_BLOCK), F32)],
        compiler_params=_params(("parallel", "arbitrary")),
        name="attn_prompt",
    )(q, kv, win, gt, pe4, w4, cosc, sinc)


def _softmax_rows(parts):
    masked = [jnp.where(k, s, NEG_INF) for s, k in parts]
    m = masked[0].max(axis=1, keepdims=True)
    for s in masked[1:]:
        m = jnp.maximum(m, s.max(axis=1, keepdims=True))
    es = [jnp.where(k, jnp.exp2(s - m), 0.0) for s, (_, k) in zip(masked, parts)]
    l = es[0].sum(axis=1, keepdims=True)
    for e in es[1:]:
        l = l + e.sum(axis=1, keepdims=True)
    return es, jnp.maximum(l, TINY)


def _attn_sample_kernel(pt_ref, q_ref, kv_ref, win_ref, gate_ref, sw_ref, *rest, n_pages, page, past):
    del pt_ref
    n_seq = q_ref.shape[0]
    pe4_ref, w4_ref, cosc_ref, sinc_ref, expand_ref, perm_ref = rest[n_seq * n_pages:n_seq * n_pages + 6]
    o_ref, wout_all_ref = rest[-2:]
    n_prev = wout_all_ref.shape[0] - 1
    if n_prev:
        wout_all_ref[0:n_prev] = rest[-3][...]
    wout_ref = wout_all_ref.at[n_prev]
    group = perm_ref.shape[0] // page
    moved = []
    for r in range(n_seq):
        pages = rest[r * n_pages:(r + 1) * n_pages]
        for jj in range(n_pages // group):
            x_t = jnp.concatenate([pages[group * jj + k][0, 0:2 * KV_COLS, :] for k in range(group)], axis=1)
            hi = x_t.astype(BF16)
            lo = (x_t - hi.astype(F32)).astype(BF16)
            moved.append(_dot_nt(perm_ref[...], hi) + _dot_nt(perm_ref[...], lo))
    rows_at = lambda p: jnp.concatenate([m[8 * p:8 * p + 8] for m in moved], axis=0)
    kcmp, vcmp = _compress(rows_at, pe4_ref, w4_ref, cosc_ref, sinc_ref, n_seq * N_CMP)
    for r in range(n_seq):
        one = lambda a: [x[r * N_CMP:(r + 1) * N_CMP] for x in a]
        _attn_sample_seq(r, rest[r * n_pages:(r + 1) * n_pages], one(kcmp), one(vcmp), q_ref, kv_ref, win_ref,
                         gate_ref, sw_ref, expand_ref, o_ref, wout_ref, past)


def _attn_sample_seq(r, pages, kcmp, vcmp, q_ref, kv_ref, win_ref, gate_ref, sw_ref, expand_ref, o_ref, wout_ref,
                     past):
    n_pages = len(pages)
    nq = q_ref.shape[1]
    wb = sw_ref.shape[2]
    new_pad = 16
    cached = lambda r0: jnp.concatenate([pages[j][0, r0:r0 + HD, :].astype(BF16) for j in range(n_pages)], axis=1)
    kt = [cached(2 * KV_COLS + g * HD) for g in range(KVH)]
    vt = [cached(3 * KV_COLS + g * HD) for g in range(KVH)]

    qf = q_ref[r]
    kvn = kv_ref[r]
    wn = win_ref[r]
    gates = gate_ref[r]
    wout_ref[r, :, 0:wb - nq] = sw_ref[r, :, nq:wb]
    wout_ref[r, :, wb - nq:wb] = wn.T

    groups = range(KVH)
    rows = H_A * nq
    grp = lambda a, g: a[g * HPG * nq:(g + 1) * HPG * nq]
    per_group = lambda f: jnp.concatenate([f(g) for g in groups], axis=0)
    zpad = jnp.zeros((new_pad - nq, HD), F32)
    new_rows = lambda a: jnp.concatenate([a, zpad], axis=0).astype(BF16)
    ns_real = past // L_SEL + 1
    iq = lambda n: lax.broadcasted_iota(jnp.int32, (rows, n), 0) % nq
    ik = lambda n: lax.broadcasted_iota(jnp.int32, (rows, n), 1)
    causal_new = (ik(new_pad) <= iq(new_pad)) & (ik(new_pad) < nq)
    qg = [jnp.concatenate([qf[:, (g * HPG + h) * HD:(g * HPG + h + 1) * HD] for h in range(HPG)],
                          axis=0).astype(BF16) for g in groups]
    m_c = (ik(N_CMP) * L_CMP + (L_CMP - 1)) <= past + iq(N_CMP)
    (e_c,), l_c = _softmax_rows([(per_group(lambda g: _dot_nt(qg[g], kcmp[g])), m_c)])
    p_c = e_c / l_c
    o_c = per_group(lambda g: _dot(grp(p_c, g).astype(BF16), vcmp[g]))
    imp = per_group(lambda g: sum(p_c[(g * HPG + h) * nq:(g * HPG + h + 1) * nq] for h in range(HPG)))
    imp = jnp.concatenate([imp, jnp.zeros((KVH * nq, 128 - N_CMP), F32)], axis=1)
    imp2 = imp + pltpu.roll(imp, 127, 1)
    lane = lax.broadcasted_iota(jnp.int32, (KVH * nq, 128), 1)
    blk = lane // 2
    t_sel = past + lax.broadcasted_iota(jnp.int32, (KVH * nq, 128), 0) % nq
    cur = t_sel // L_SEL
    forced = (blk == 0) | (blk == cur) | (blk == cur - 1)
    valid = (blk * L_SEL) <= t_sel
    score = jnp.where(valid, imp2 + jnp.where(forced, FORCE_BONUS, 0.0), -FORCE_BONUS)
    score = jnp.where(((lane & 1) == 0) & (blk < ns_real), score, -jnp.inf)
    cnt = jnp.zeros((KVH * nq, 128), jnp.int32)
    for k in range(ns_real):
        ck = score[:, 2 * k:2 * k + 1]
        beats = (ck > score) | ((ck == score) & (lane > 2 * k))
        cnt = cnt + beats.astype(jnp.int32)
    sel = (cnt < N_SEL).astype(BF16)
    sel_rows = per_group(lambda g: jnp.concatenate([sel[g * nq:(g + 1) * nq]] * HPG, axis=0))
    picked = _dot(sel_rows, expand_ref[...]) > 0.5
    s_p = per_group(lambda g: _dot(qg[g], kt[g]))
    s_n = per_group(lambda g: _dot_nt(qg[g], new_rows(kvn[:, 2 * KV_COLS + g * HD:2 * KV_COLS + (g + 1) * HD])))
    (e_p, e_n), l_s = _softmax_rows([(s_p, picked[:, 0:past]), (s_n, picked[:, past:past + new_pad] & causal_new)])
    o_s = per_group(lambda g: _dot_nt(grp(e_p, g).astype(BF16), vt[g]) + _dot(
        grp(e_n, g).astype(BF16), new_rows(kvn[:, 3 * KV_COLS + g * HD:3 * KV_COLS + (g + 1) * HD]))) / l_s
    s_wp = per_group(lambda g: _dot(qg[g], sw_ref[r, g * HD:(g + 1) * HD, :].astype(BF16)))
    s_wn = per_group(lambda g: _dot_nt(qg[g], new_rows(wn[:, g * HD:(g + 1) * HD])))
    (e_wp, e_wn), l_w = _softmax_rows([(s_wp, ik(wb) > iq(wb) + (wb - WINDOW)), (s_wn, causal_new)])
    o_w = per_group(lambda g: _dot_nt(
        grp(e_wp, g).astype(BF16), sw_ref[r, KV_COLS + g * HD:KV_COLS + (g + 1) * HD, :].astype(BF16)) + _dot(
        grp(e_wn, g).astype(BF16), new_rows(wn[:, KV_COLS + g * HD:KV_COLS + (g + 1) * HD]))) / l_w
    gate = lambda br: jnp.concatenate([gates[:, 3 * hd + br:3 * hd + br + 1] for hd in range(H_A)], axis=0)
    o = gate(0) * o_c + gate(1) * o_s + gate(2) * o_w
    for pair in range(H_A // 2):
        o_ref[r, :, 2 * pair * HD:(2 * pair + 2) * HD] = jnp.concatenate(
            [o[2 * pair * nq:(2 * pair + 1) * nq], o[(2 * pair + 1) * nq:(2 * pair + 2) * nq]], axis=1)


def _attn_sample(page_ids, win_off, q3, kv3, win3, gates3, state_win_t, cache_t, pe4, w4, cosc, sinc, expand, perm,
                 wout_prev):
    nb, nq, _ = q3.shape
    n_prev = 0 if wout_prev is None else wout_prev.shape[0]
    n_pages = page_ids.shape[1]
    page = cache_t.shape[2]
    past = n_pages * page
    wb = state_win_t.shape[2]
    ns = SAMPLE_SEQS
    wout_spec = lambda n: pl.BlockSpec((n, ns, 2 * KV_COLS, wb), lambda b, pt: (0, b, 0, 0))
    seq3 = lambda r, w: pl.BlockSpec((ns, r, w), lambda b, pt: (b, 0, 0))
    page_spec = lambda r, j: pl.BlockSpec((1, 4 * KV_COLS, page), lambda b, pt: (pt[ns * b + r, j], 0, 0))
    const = lambda a: pl.BlockSpec(a.shape, lambda b, pt: (0,) * a.ndim, pipeline_mode=pl.Buffered(1))
    grid_spec = pltpu.PrefetchScalarGridSpec(
        num_scalar_prefetch=1,
        grid=(nb // ns,),
        in_specs=[seq3(nq, Q_COLS), seq3(nq, 4 * KV_COLS), seq3(nq, 2 * KV_COLS), seq3(nq, 128),
                  pl.BlockSpec((ns, 2 * KV_COLS, wb), lambda b, pt: (win_off // ns + b, 0, 0))]
        + [page_spec(r, j) for r in range(ns) for j in range(n_pages)]
        + [const(pe4), const(w4), const(cosc), const(sinc), const(expand), const(perm)]
        + [wout_spec(n_prev)] * bool(n_prev),
        out_specs=[seq3(nq, Q_COLS), wout_spec(n_prev + 1)],
    )
    return pl.pallas_call(
        functools.partial(_attn_sample_kernel, n_pages=n_pages, page=page, past=past),
        grid_spec=grid_spec,
        out_shape=[jax.ShapeDtypeStruct((nb, nq, Q_COLS), F32),
                   jax.ShapeDtypeStruct((n_prev + 1, nb, 2 * KV_COLS, wb), F32)],
        compiler_params=_params(("arbitrary",)),
        name="attn_sample",
    )(page_ids, q3, kv3, win3, gates3, state_win_t, *([cache_t] * (ns * n_pages)), pe4, w4, cosc, sinc, expand, perm,
      *([wout_prev] * bool(n_prev)))


def _conv_kernel(ue_ref, w_ref, b_ref, g_ref, bb_ref, c_ref, shift_scr, *, rows):
    nb, st, ch = c_ref.shape
    pad = CONV_HALO - (CONV_W - 1)
    for r0 in range(0, st, rows):
        acc = jnp.zeros((nb, rows, ch), F32)
        for res in range(8):
            taps = range(res, CONV_W, 8)
            span = 8 * (len(taps) - 1) + rows
            shift_scr[res, :, 0:span, :] = ue_ref[:, pl.ds(pad + r0 + res, span), :]
            for m, k in enumerate(taps):
                acc = acc + shift_scr[res, :, 8 * m:8 * m + rows, :] * w_ref[k:k + 1, :].reshape(1, 1, ch)
        y = _layer_norm(acc + b_ref[...].reshape(1, 1, ch), g_ref[...].reshape(1, 1, ch), bb_ref[...].reshape(1, 1, ch))
        c_ref[:, r0:r0 + rows, :] = y * _sigmoid(y)


def _with_history(body, halo):
    def kernel_fn(cur_ref, prev_ref, *rest):
        ue_scr = rest[-1]
        first = pl.program_id(1) == 0
        ue_scr[:, 0:halo, :] = jnp.where(first, 0.0, prev_ref[...])
        ue_scr[:, halo:, :] = cur_ref[...]
        body(ue_scr, *rest[:-1])
    return kernel_fn


def _conv_shift_scratch(nb, rows, ch):
    return pltpu.VMEM((8, nb, 8 * ((CONV_W - 1) // 8) + rows, ch), F32)


def _halo_specs(ts, halo, ch):
    cur = pl.BlockSpec((1, ts, ch), lambda b, i: (b, i, 0))
    prev = pl.BlockSpec((1, halo, ch), lambda b, i: (b, jnp.maximum(i * (ts // halo) - 1, 0), 0))
    return cur, prev


def _conv_module_seq(u3, w, b, g, bb, ts):
    nseq, s, ch = u3.shape
    cur, prev = _halo_specs(ts, CONV_HALO, ch)
    consts = (w, b, g, bb)
    return pl.pallas_call(
        _with_history(functools.partial(_conv_kernel, rows=32), CONV_HALO),
        grid=(nseq, s // ts),
        in_specs=[cur, prev] + [pl.BlockSpec(a.shape, lambda bi, i: (0, 0), pipeline_mode=pl.Buffered(1)) for a in consts],
        out_specs=cur,
        out_shape=jax.ShapeDtypeStruct((nseq, s, ch), F32),
        scratch_shapes=[_conv_shift_scratch(1, 32, ch), pltpu.VMEM((1, ts + CONV_HALO, ch), F32)],
        compiler_params=_params(("parallel", "arbitrary")),
        name="conv_module_seq",
    )(u3, u3, *consts)


def _conv_module(ue_tiles, w, b, g, bb, nb, st):
    n = ue_tiles.shape[0]
    ch = ue_tiles.shape[2]
    blk = lambda r: pl.BlockSpec((nb, r, ch), lambda i: (i, 0, 0))
    return pl.pallas_call(
        functools.partial(_conv_kernel, rows=min(st, 32)),
        grid=(n // nb,),
        in_specs=[blk(st + CONV_HALO)] + [_const_spec(a.shape) for a in (w, b, g, bb)],
        out_specs=blk(st),
        out_shape=jax.ShapeDtypeStruct((n, st, ch), F32),
        scratch_shapes=[_conv_shift_scratch(nb, min(st, 32), ch)],
        compiler_params=_params(("parallel",)),
        name="conv_module",
    )(ue_tiles, w, b, g, bb)


def _matmul_kernel(x_ref, w_ref, o_ref):
    o_ref[...] = _dot(x_ref[...].astype(BF16), w_ref[...])


def _matmul(x, w, tm):
    t, k = x.shape
    n = w.shape[1]
    return pl.pallas_call(
        _matmul_kernel,
        grid=(t // tm,),
        in_specs=[pl.BlockSpec((tm, k), lambda i: (i, 0)), _const_spec(w.shape)],
        out_specs=pl.BlockSpec((tm, n), lambda i: (i, 0)),
        out_shape=jax.ShapeDtypeStruct((t, n), F32),
        compiler_params=_params(("parallel",)),
        name="pool_inproj",
    )(x, w)


def _pool_kernel(ue_ref, d_ref, *, rows, tile_axis, pos0):
    nb, st, _ = d_ref.shape
    tile = 0 if tile_axis is None else pl.program_id(tile_axis)
    for r0 in range(0, st, rows):
        pos = pos0 + tile * st + r0 + lax.broadcasted_iota(jnp.int32, (1, rows, G_C), 1)
        for g, wdw in enumerate(POOL_WINDOWS):
            cols = slice(g * G_C, (g + 1) * G_C)
            acc = ue_ref[:, pl.ds(POOL_HALO + r0, rows), cols]
            u = acc
            for k in range(1, wdw):
                acc = acc + ue_ref[:, pl.ds(POOL_HALO + r0 - k, rows), cols]
            cnt = jnp.minimum(pos + 1, wdw).astype(F32)
            d_ref[:, r0:r0 + rows, cols] = acc / cnt - u


def _pool_diff_seq(u3, ts):
    nseq, s, ch = u3.shape
    cur, prev = _halo_specs(ts, POOL_HALO, ch)
    return pl.pallas_call(
        _with_history(functools.partial(_pool_kernel, rows=64, tile_axis=1, pos0=0), POOL_HALO),
        grid=(nseq, s // ts),
        in_specs=[cur, prev],
        out_specs=cur,
        out_shape=jax.ShapeDtypeStruct((nseq, s, ch), F32),
        scratch_shapes=[pltpu.VMEM((1, ts + POOL_HALO, ch), F32)],
        compiler_params=_params(("parallel", "arbitrary")),
        name="pool_diff_seq",
    )(u3, u3)


def _pool_diff(ue_tiles, nb, st, pos0):
    n = ue_tiles.shape[0]
    ch = ue_tiles.shape[2]
    blk = lambda r: pl.BlockSpec((nb, r, ch), lambda i: (i, 0, 0))
    return pl.pallas_call(
        functools.partial(_pool_kernel, rows=min(st, 64), tile_axis=None, pos0=pos0),
        grid=(n // nb,),
        in_specs=[blk(st + POOL_HALO)],
        out_specs=blk(st),
        out_shape=jax.ShapeDtypeStruct((n, st, ch), F32),
        compiler_params=_params(("parallel",)),
        name="pool_diff",
    )(ue_tiles)


def _ffn_ln(x1, w1_ref, w3_ref, w2_ref, g2_ref, b2_ref):
    xb = x1.astype(BF16)
    h1 = _dot(xb, w1_ref[...])
    h3 = _dot(xb, w3_ref[...])
    gated = (h1 * _sigmoid(h1) * h3).astype(BF16)
    return _layer_norm(ALPHA * x1 + _dot(gated, w2_ref[...]), g2_ref[...], b2_ref[...])


def _post_a_kernel(x_ref, oa_ref, c_ref, wo1_ref, wo2_ref, g1_ref, b1_ref, w1_ref, w3_ref, w2_ref, g2_ref, b2_ref,
                   o_ref):
    y = _dot(oa_ref[...].astype(BF16), wo1_ref[...]) + _dot(c_ref[...].astype(BF16), wo2_ref[...])
    x1 = _layer_norm(ALPHA * x_ref[...] + y, g1_ref[...], b1_ref[...])
    o_ref[...] = _ffn_ln(x1, w1_ref, w3_ref, w2_ref, g2_ref, b2_ref)


def _post_c_kernel(x_ref, d_ref, wg_ref, sc_ref, wo_ref, g1_ref, b1_ref, w1_ref, w3_ref, w2_ref, g2_ref, b2_ref,
                   o_ref):
    db = d_ref[...].astype(BF16)
    z = jnp.concatenate([_dot(db[:, g * G_C:(g + 1) * G_C], wg_ref[g]) for g in range(len(POOL_WINDOWS))], axis=1)
    y = _dot((z * sc_ref[...]).astype(BF16), wo_ref[...])
    x1 = _layer_norm(ALPHA * x_ref[...] + y, g1_ref[...], b1_ref[...])
    o_ref[...] = _ffn_ln(x1, w1_ref, w3_ref, w2_ref, g2_ref, b2_ref)


def _post(body, x, acts, consts, tm):
    t = x.shape[0]
    row = lambda a: pl.BlockSpec((tm, a.shape[1]), lambda i: (i, 0))
    return pl.pallas_call(
        body,
        grid=(t // tm,),
        in_specs=[row(x)] + [row(a) for a in acts] + [_const_spec(c.shape) for c in consts],
        out_specs=row(x),
        out_shape=jax.ShapeDtypeStruct(x.shape, F32),
        compiler_params=_params(("parallel",)),
        name=body.__name__.strip("_"),
    )(x, *acts, *consts)


def _rope_tables(pos):
    half = HD // 2
    inv = ROPE_THETA ** (-jnp.arange(half, dtype=F32) / half)
    ang = pos.astype(F32)[:, None] * inv[None, :]
    cos, sin = jnp.cos(ang), jnp.sin(ang)
    return jnp.tile(cos, (1, 4)), jnp.tile(jnp.concatenate([-sin, sin], axis=1), (1, 2))


def _with_hist_rows(u3, hist, halo):
    b, _, c = u3.shape
    return jnp.concatenate([jnp.zeros((b, halo - hist.shape[1], c), u3.dtype), hist, u3], axis=1)


def _prep_weights(p):
    n_a = p["w_in_a"].shape[0]
    cuts = np.cumsum([0, Q_COLS, 4 * KV_COLS, 2 * KV_COLS, GATE_COLS, C_B, C_B])
    out = {"a": [], "c": [], "ffn": []}
    for ia in range(n_a):
        w_in = p["w_in_a"][ia]
        wq, wkv, wwin, wgl, wga, wgb = [w_in[:, cuts[k]:cuts[k + 1]].astype(BF16) for k in range(6)]
        wglt = jnp.pad(wgl.T, ((0, GATE_ROWS - GATE_COLS), (0, 0)))
        wglr = jnp.pad(wgl, ((0, 0), (0, 128 - GATE_COLS)))
        wk, wv = p["w_cmp_k"][ia].astype(BF16), p["w_cmp_v"][ia].astype(BF16)
        w4 = jnp.zeros((L_CMP, 4 * HD, 4 * HD), BF16)
        for k, blk in enumerate((wk, wk, wv, wv)):
            w4 = w4.at[:, k * HD:(k + 1) * HD, k * HD:(k + 1) * HD].set(blk)
        pe4 = jnp.concatenate([p["pe_cmp_k"][ia]] * 2 + [p["pe_cmp_v"][ia]] * 2, axis=1)
        wo = p["w_out_a"][ia].astype(BF16)
        out["a"].append(dict(
            inproj_t=(wq, wkv, wwin, wga, wgb, wglt), inproj_r=(wq, wkv, wwin, wga, wgb, wglr), w4=w4, pe4=pe4,
            conv=(jnp.pad(p["conv_w"][ia], ((0, 1), (0, 0))), p["conv_b"][ia][None], p["conv_ln_g"][ia][None],
                  p["conv_ln_b"][ia][None]),
            wo1=wo[:Q_COLS], wo2=wo[Q_COLS:]))
    for ic in range(p["w_in_c"].shape[0]):
        out["c"].append(dict(w_in=p["w_in_c"][ic].astype(BF16), w_grp=p["w_grp_c"][ic].astype(BF16),
                             scale=p["scale_c"][ic][None], w_out=p["w_out_c"][ic].astype(BF16)))
    for l in range(DEPTH):
        out["ffn"].append((p["ln1_g"][l][None], p["ln1_b"][l][None], p["w1"][l].astype(BF16),
                           p["w3"][l].astype(BF16), p["w2"][l].astype(BF16), p["ln2_g"][l][None], p["ln2_b"][l][None]))
    return out


def _trunk(x3, pos0, wb, wts, cache_kv, page_table, state_win, state_conv, state_pool):
    b, s, _ = x3.shape
    t = b * s
    prompt = cache_kv is None
    tm = min(512, t)
    x = x3.reshape(t, D_MODEL)
    cos, sin = _rope_tables(pos0 + jnp.arange(s, dtype=jnp.int32))
    if not prompt:
        cos, sin = jnp.tile(cos, (tm // s, 1)), jnp.tile(sin, (tm // s, 1))
    cosc, sinc = _rope_tables(jnp.arange(N_CMP, dtype=jnp.int32) * L_CMP + (L_CMP - 1))
    ts = 512 if prompt else s
    nb = 1 if prompt else 8
    if not prompt:
        n_a, n_pool, page = cache_kv.shape[:3]
        cache_t = jnp.transpose(cache_kv, (0, 1, 3, 4, 5, 2)).reshape(n_a * n_pool, 4 * KV_COLS, page)
        sw_t = jnp.transpose(state_win, (0, 1, 3, 4, 5, 2)).reshape(n_a * b, 2 * KV_COLS, wb)
        lane_blk = jnp.arange(128)[:, None]
        key_blk = jnp.arange(page_table.shape[1] * page + 128)[None, :] // L_SEL
        expand = ((lane_blk % 2 == 0) & (lane_blk // 2 == key_blk)).astype(BF16)
        out_row = jnp.arange(8 * L_CMP)[:, None]
        position = jnp.arange(8 * L_CMP)[None, :]
        perm = ((out_row // 8 == position % L_CMP) & (out_row % 8 == position // L_CMP)).astype(BF16)
    kvt_all, wout_all, win_new, conv_new, pool_new = None, None, [], [], []
    for l in range(DEPTH):
        ln1_g, ln1_b, w1, w3, w2, ln2_g, ln2_b = wts["ffn"][l]
        if l % 2 == 0:
            a = wts["a"][l // 2]
            q, kv, win, u, gates, kvt_all = _inproj_a(x, a["inproj_t" if prompt else "inproj_r"], cos, sin, tm,
                                                      prompt, kvt_all, s)
            if prompt:
                oa = _attn_prompt(q, kv, win, gates, a["pe4"], a["w4"], cosc, sinc, b, s)
                win_new.append(win.reshape(b, s, 2, KVH, HD)[:, s - wb:])
                hist = None
            else:
                ia = l // 2
                n_pool = cache_kv.shape[1]
                oa, wout_all = _attn_sample(page_table + ia * n_pool, ia * b, q.reshape(b, s, Q_COLS),
                                            kv.reshape(b, s, 4 * KV_COLS), win.reshape(b, s, 2 * KV_COLS),
                                            gates.reshape(b, s, 128), sw_t, cache_t, a["pe4"], a["w4"], cosc, sinc,
                                            expand, perm, wout_all)
                oa = oa.reshape(t, Q_COLS)
                hist = state_conv[ia]
            u3 = u.reshape(b, s, C_B)
            keep = CONV_W - 1
            conv_new.append(u3[:, s - keep:] if hist is None else jnp.concatenate([hist, u3], axis=1)[:, -keep:])
            if prompt:
                c = _conv_module_seq(u3, *a["conv"], ts).reshape(t, C_B)
            else:
                c = _conv_module(_with_hist_rows(u3, hist, CONV_HALO), *a["conv"], nb, s).reshape(t, C_B)
            x = _post(_post_a_kernel, x, (oa, c), (a["wo1"], a["wo2"], ln1_g, ln1_b, w1, w3, w2, ln2_g, ln2_b), tm)
        else:
            cw = wts["c"][l // 2]
            u3 = _matmul(x, cw["w_in"], tm).reshape(b, s, D_MODEL)
            hist = None if prompt else state_pool[l // 2]
            if prompt:
                d = _pool_diff_seq(u3, ts).reshape(t, D_MODEL)
            else:
                d = _pool_diff(_with_hist_rows(u3, hist, POOL_HALO), nb, s, pos0).reshape(t, D_MODEL)
            pool_new.append(u3[:, s - POOL_HIST:] if hist is None
                            else jnp.concatenate([hist, u3], axis=1)[:, -POOL_HIST:])
            x = _post(_post_c_kernel, x, (d,),
                      (cw["w_grp"], cw["scale"], cw["w_out"], ln1_g, ln1_b, w1, w3, w2, ln2_g, ln2_b), tm)
    kv_out = jnp.moveaxis(kvt_all, 2, 3).reshape(-1, b, s, 4, KVH, HD)
    if prompt:
        win_out = jnp.stack(win_new)
    else:
        win_out = wout_all.reshape(-1, b, 2, KVH, HD, wb).transpose(0, 1, 5, 2, 3, 4)
    return x.reshape(b, s, D_MODEL), kv_out, win_out, jnp.stack(conv_new), jnp.stack(pool_new)


def kernel(x_prompt, x_sample, cache_kv, state_win, state_conv, state_pool, page_table, w_in_a, w_cmp_k, pe_cmp_k, w_cmp_v, pe_cmp_v, conv_w, conv_b, conv_ln_g, conv_ln_b, w_out_a, w_in_c, w_grp_c, scale_c, w_out_c, ln1_g, ln1_b, ln2_g, ln2_b, w1, w3, w2):
    p = dict(w_in_a=w_in_a, w_cmp_k=w_cmp_k, pe_cmp_k=pe_cmp_k, w_cmp_v=w_cmp_v, pe_cmp_v=pe_cmp_v,
             conv_w=conv_w, conv_b=conv_b, conv_ln_g=conv_ln_g, conv_ln_b=conv_ln_b, w_out_a=w_out_a,
             w_in_c=w_in_c, w_grp_c=w_grp_c, scale_c=scale_c, w_out_c=w_out_c,
             ln1_g=ln1_g, ln1_b=ln1_b, ln2_g=ln2_g, ln2_b=ln2_b, w1=w1, w3=w3, w2=w2)
    wts = _prep_weights(p)
    past_len = page_table.shape[1] * cache_kv.shape[2]
    wb = state_win.shape[2]
    y_p, kv_p, win_p, conv_p, pool_p = _trunk(x_prompt, 0, wb, wts, None, None, None, None, None)
    y_s, kv_s, win_s, conv_s, pool_s = _trunk(x_sample, past_len, wb, wts, cache_kv, page_table, state_win,
                                              state_conv, state_pool)
    return (y_p, y_s, kv_p, kv_s, win_p, win_s, conv_p, conv_s, pool_p, pool_s)
```

```python
import functools

import numpy as np
import jax
import jax.numpy as jnp
from jax import lax
from jax.experimental import pallas as pl
from jax.experimental.pallas import tpu as pltpu

F32 = jnp.float32
BF16 = jnp.bfloat16

D_MODEL = 1024
DEPTH = 4
HD = 64
H_A = 8
KVH = 2
HPG = H_A // KVH
L_CMP = 32
L_SEL = 64
N_SEL = 8
WINDOW = 512
ROPE_THETA = 10000.0
FORCE_BONUS = 1.0e4
NEG_INF = -1.0e30
TINY = 1.0e-30
C_B = D_MODEL // 2
CONV_W = 31
POOL_WINDOWS = (2, 4, 8, 16)
G_C = D_MODEL // len(POOL_WINDOWS)
POOL_HIST = max(POOL_WINDOWS) - 1
D_FF = 2816
ALPHA = (2 * DEPTH) ** 0.25
Q_COLS = H_A * HD
KV_COLS = KVH * HD
GATE_COLS = 3 * H_A
GATE_ROWS = 32
Q_BLOCK = 256
N_CMP = 64
SEL_CHUNK = 256
WIN_SPAN = WINDOW + Q_BLOCK
SAMPLE_SEQS = 2
CONV_HALO = 32
POOL_HALO = 16
VMEM_LIMIT = 56 * 1024 * 1024
LOG2E = 1.4426950408889634
Q_SCALE = HD ** -0.5 * LOG2E


def _dot(a, b):
    return jnp.dot(a, b, preferred_element_type=F32)


def _dot_nt(a, b):
    return lax.dot_general(a, b, (((1,), (1,)), ((), ())), preferred_element_type=F32)


def _dot_tn(a, b):
    return lax.dot_general(a, b, (((0,), (0,)), ((), ())), preferred_element_type=F32)


def _sigmoid(x):
    return 1.0 / (1.0 + jnp.exp(-x))


def _layer_norm(y, g, b):
    mu = jnp.mean(y, axis=-1, keepdims=True)
    d = y - mu
    var = jnp.mean(d * d, axis=-1, keepdims=True)
    return d * lax.rsqrt(var + 1e-5) * g + b


def _rope128(v, cos, sin_signed):
    lane = lax.broadcasted_iota(jnp.int32, v.shape, 1)
    rot = jnp.where((lane % HD) < HD // 2, pltpu.roll(v, 128 - HD // 2, 1), pltpu.roll(v, HD // 2, 1))
    return v * cos + rot * sin_signed


def _params(sem=None):
    return pltpu.CompilerParams(dimension_semantics=sem, vmem_limit_bytes=VMEM_LIMIT)


def _const_spec(shape):
    nd = len(shape)
    return pl.BlockSpec(shape, lambda *_: (0,) * nd, pipeline_mode=pl.Buffered(1))


def _inproj_a_kernel(x_ref, wq_ref, wkv_ref, wwin_ref, wga_ref, wgb_ref, wgl_ref, cos_ref, sin_ref, *rest,
                     gates_transposed, n_prev):
    q_out, kv_out, win_out, u_out, g_out, kvt_all_out = rest[-6:]
    if n_prev:
        kvt_all_out[0:n_prev] = rest[0][...]
    xb = x_ref[...].astype(BF16)
    cos = cos_ref[...]
    sin = sin_ref[...]
    q = _dot(xb, wq_ref[...])
    for c in range(Q_COLS // 128):
        q_out[:, c * 128:(c + 1) * 128] = _rope128(q[:, c * 128:(c + 1) * 128], cos, sin) * Q_SCALE
    kv = _dot(xb, wkv_ref[...])
    kv_out[:, 0:256] = kv[:, 0:256]
    kv_out[:, 256:384] = _rope128(kv[:, 256:384], cos, sin)
    kv_out[:, 384:512] = kv[:, 384:512]
    kvt_all_out[n_prev, 0] = kv_out[...].T
    w = _dot(xb, wwin_ref[...])
    win_out[:, 0:128] = _rope128(w[:, 0:128], cos, sin)
    win_out[:, 128:256] = w[:, 128:256]
    u_out[...] = _dot(xb, wga_ref[...]) * _sigmoid(_dot(xb, wgb_ref[...]))
    g_out[...] = _sigmoid(_dot_nt(wgl_ref[...], xb) if gates_transposed else _dot(xb, wgl_ref[...]))


def _inproj_a(x, wts, cos, sin, tm, gates_transposed, kvt_prev, seq):
    t = x.shape[0]
    n_prev = 0 if kvt_prev is None else kvt_prev.shape[0]
    nt = max(seq // tm, 1)
    kvt_spec = lambda n: pl.BlockSpec((n, 1, 4 * KV_COLS, tm), lambda i: (0, i // nt, 0, i % nt))
    kvt_cols = max(seq, tm)
    g_spec = pl.BlockSpec((GATE_ROWS, tm), lambda i: (0, i)) if gates_transposed else pl.BlockSpec((tm, 128), lambda i: (i, 0))
    g_shape = (GATE_ROWS, t) if gates_transposed else (t, 128)
    n_tab = cos.shape[0] // tm
    row = lambda w: pl.BlockSpec((tm, w), lambda i: (i, 0))
    tab = pl.BlockSpec((tm, 128), lambda i: (i % n_tab, 0))
    return pl.pallas_call(
        functools.partial(_inproj_a_kernel, gates_transposed=gates_transposed, n_prev=n_prev),
        grid=(t // tm,),
        in_specs=[row(D_MODEL)] + [_const_spec(w.shape) for w in wts] + [tab, tab] + [kvt_spec(n_prev)] * bool(n_prev),
        out_specs=[row(Q_COLS), row(4 * KV_COLS), row(2 * KV_COLS), row(C_B), g_spec, kvt_spec(n_prev + 1)],
        out_shape=[jax.ShapeDtypeStruct((t, Q_COLS), F32), jax.ShapeDtypeStruct((t, 4 * KV_COLS), F32),
                   jax.ShapeDtypeStruct((t, 2 * KV_COLS), F32), jax.ShapeDtypeStruct((t, C_B), F32),
                   jax.ShapeDtypeStruct(g_shape, F32),
                   jax.ShapeDtypeStruct((n_prev + 1, t // kvt_cols, 4 * KV_COLS, kvt_cols), F32)],
        compiler_params=_params(("parallel",)),
        name="inproj_a",
    )(x, *wts, cos, sin, *([kvt_prev] * bool(n_prev)))


def _compress(rows_at, pe4_ref, w4_ref, cosc_ref, sinc_ref, n_blocks=N_CMP):
    acc = jnp.zeros((n_blocks, 4 * HD), F32)
    for p in range(L_CMP):
        xp = rows_at(p) + pe4_ref[p:p + 1, :]
        acc = acc + _dot(xp.astype(BF16), w4_ref[p])
    reps = n_blocks // N_CMP
    k = _rope128(acc[:, 0:2 * HD], jnp.concatenate([cosc_ref[...]] * reps, axis=0),
                 jnp.concatenate([sinc_ref[...]] * reps, axis=0))
    v = acc[:, 2 * HD:4 * HD]
    return ([k[:, g * HD:(g + 1) * HD].astype(BF16) for g in range(KVH)],
            [v[:, g * HD:(g + 1) * HD].astype(BF16) for g in range(KVH)])


def _softmax_t(s, msk):
    s = jnp.where(msk, s, NEG_INF)
    m = jnp.max(s, axis=0, keepdims=True)
    p = jnp.where(msk, jnp.exp2(s - m), 0.0)
    return m, p, jnp.sum(p, axis=0, keepdims=True)


def _nsa_core_t(qf, gt, t_q, t0, kcmp, vcmp, ksel, vselt, kwin, vwint, imp_scr, bias_scr, n_sel_blocks):
    nq = Q_BLOCK
    w = HPG * nq
    groups = range(KVH)
    t_lane = jnp.concatenate([t_q] * HPG, axis=1)
    row_c = lax.broadcasted_iota(jnp.int32, (N_CMP, w), 0)
    key_row = lax.broadcasted_iota(jnp.int32, (nq, nq), 0)
    heads = lambda a: jnp.concatenate([a] * HPG, axis=1)
    sel_per_chunk = SEL_CHUNK // L_SEL
    n_full = t0 // SEL_CHUNK
    win_lo = pl.multiple_of(jnp.maximum(t0 + nq - WIN_SPAN, 0), nq)
    causal_bias = heads(jnp.where(t0 + key_row <= t_q, 0.0, NEG_INF))
    win_bias = []
    for k in range(WIN_SPAN // nq):
        kpos = win_lo + k * nq + key_row
        ok = kpos <= t_q
        if k == 0:
            ok = ok & (kpos > t_q - WINDOW)
        win_bias.append(heads(jnp.where(ok, 0.0, NEG_INF)))
    qgs = [jnp.concatenate([qf[:, (g * HPG + h) * HD:(g * HPG + h + 1) * HD] for h in range(HPG)],
                           axis=0).astype(BF16) for g in groups]
    o_c = []
    for g in groups:
        m_c = (row_c * L_CMP + (L_CMP - 1)) <= t_lane
        _, e_c, l_c = _softmax_t(_dot_nt(kcmp[g], qgs[g]), m_c)
        p_c = e_c / jnp.maximum(l_c, TINY)
        o_c.append(_dot_tn(vcmp[g], p_c.astype(BF16)))
        imp = p_c[:, 0:nq]
        for h in range(1, HPG):
            imp = imp + p_c[:, h * nq:(h + 1) * nq]
        half = N_CMP // 2
        pair_sums = []
        for c in range(nq // 128):
            slot = g * (nq // 128) + c
            imp_scr[slot] = imp[:, c * 128:(c + 1) * 128]
            pair_sums.append(imp_scr[slot, pl.ds(0, half, stride=2), :] + imp_scr[slot, pl.ds(1, half, stride=2), :])
        imp2 = jnp.concatenate(pair_sums, axis=1)
        blk = lax.broadcasted_iota(jnp.int32, (n_sel_blocks, nq), 0)
        cur = t_q // L_SEL
        forced = (blk == 0) | (blk == cur) | (blk == cur - 1)
        valid = (blk * L_SEL) <= t_q
        score = jnp.where(valid, imp2 + jnp.where(forced, FORCE_BONUS, 0.0), -FORCE_BONUS)
        cnt = jnp.zeros((n_sel_blocks, nq), jnp.int32)
        for k in range(n_sel_blocks):
            rk = score[k:k + 1, :]
            beats = (rk > score) | ((rk == score) & (blk > k))
            cnt = cnt + beats.astype(jnp.int32)
        bias_scr[g] = jnp.concatenate([jnp.where(cnt < N_SEL, 0.0, NEG_INF)] * HPG, axis=1)

    def sel_chunk(j, carries, diagonal):
        off = pl.multiple_of(j * SEL_CHUNK, SEL_CHUNK)
        out = []
        for g in groups:
            m, l, acc = carries[g]
            s = _dot_nt(ksel[g, pl.ds(off, SEL_CHUNK), :], qgs[g])
            s = jnp.concatenate([s[r * L_SEL:(r + 1) * L_SEL] + bias_scr[g, pl.ds(j * sel_per_chunk + r, 1), :]
                                 for r in range(sel_per_chunk)], axis=0)
            if diagonal:
                s = s + causal_bias
            m_j = jnp.max(s, axis=0, keepdims=True)
            p = jnp.exp2(s - m_j)
            m_new = jnp.maximum(m, m_j)
            a_old = jnp.exp2(m - m_new)
            a_new = jnp.exp2(m_j - m_new)
            pv = _dot(vselt[g, :, pl.ds(off, SEL_CHUNK)], p.astype(BF16))
            out.append((m_new, a_old * l + a_new * jnp.sum(p, axis=0, keepdims=True), a_old * acc + a_new * pv))
        return tuple(out)

    init = tuple((jnp.full((1, w), NEG_INF, F32), jnp.zeros((1, w), F32), jnp.zeros((HD, w), F32)) for _ in groups)
    carries = lax.fori_loop(0, n_full, functools.partial(sel_chunk, diagonal=False), init)
    carries = sel_chunk(n_full, carries, diagonal=True)
    outs = []
    for g in groups:
        _, l_s, a_s = carries[g]
        o_s = a_s / l_s
        s_w = _dot_nt(kwin[g, pl.ds(win_lo, WIN_SPAN), :], qgs[g])
        s_w = jnp.concatenate([s_w[k * nq:(k + 1) * nq] + win_bias[k] for k in range(WIN_SPAN // nq)], axis=0)
        p_w = jnp.exp2(s_w - jnp.max(s_w, axis=0, keepdims=True))
        o_w = (_dot(vwint[g, :, pl.ds(win_lo, WIN_SPAN)], p_w.astype(BF16))
               / jnp.sum(p_w, axis=0, keepdims=True))
        for h in range(HPG):
            r = 3 * (g * HPG + h)
            sl = slice(h * nq, (h + 1) * nq)
            outs.append(gt[r:r + 1, :] * o_c[g][:, sl] + gt[r + 1:r + 2, :] * o_s[:, sl]
                        + gt[r + 2:r + 3, :] * o_w[:, sl])
    return outs


def _attn_prompt_kernel(q_ref, kv_ref, win_ref, gt_ref, pe4_ref, w4_ref, cosc_ref, sinc_ref, o_ref,
                        kcf, ksel, vselt, kwin, vwint, kcmp, vcmp, imp_scr, bias_scr, *, seq):
    i = pl.program_id(1)

    @pl.when(i == 0)
    def _():
        v_t = kv_ref[:, 3 * KV_COLS:4 * KV_COLS].T.astype(BF16)
        vw_t = win_ref[:, KV_COLS:2 * KV_COLS].T.astype(BF16)
        for g in range(KVH):
            ksel[g] = kv_ref[:, 2 * KV_COLS + g * HD:2 * KV_COLS + (g + 1) * HD].astype(BF16)
            kwin[g] = win_ref[:, g * HD:(g + 1) * HD].astype(BF16)
            vselt[g] = v_t[g * HD:(g + 1) * HD]
            vwint[g] = vw_t[g * HD:(g + 1) * HD]
        for c in range(2):
            kcf[c] = kv_ref[:, c * KV_COLS:(c + 1) * KV_COLS]
        rows_at = lambda p: jnp.concatenate([kcf[c, pl.ds(p, N_CMP, stride=L_CMP), :] for c in range(2)], axis=1)
        k_c, v_c = _compress(rows_at, pe4_ref, w4_ref, cosc_ref, sinc_ref)
        for g in range(KVH):
            kcmp[g] = k_c[g]
            vcmp[g] = v_c[g]

    t0 = i * Q_BLOCK
    t_q = t0 + lax.broadcasted_iota(jnp.int32, (1, Q_BLOCK), 1)
    outs = _nsa_core_t(q_ref[...], gt_ref[...], t_q, t0, kcmp, vcmp, ksel, vselt, kwin, vwint, imp_scr, bias_scr,
                       n_sel_blocks=seq // L_SEL)
    for pair in range(H_A // 2):
        o_ref[:, pair * 128:(pair + 1) * 128] = jnp.concatenate(outs[2 * pair:2 * pair + 2], axis=0).T


def _attn_prompt(q, kv, win, gt, pe4, w4, cosc, sinc, batch, seq):
    t = q.shape[0]
    nqb = seq // Q_BLOCK
    qrow = lambda w: pl.BlockSpec((Q_BLOCK, w), lambda b, i: (b * nqb + i, 0))
    srow = lambda w: pl.BlockSpec((seq, w), lambda b, i: (b, 0))
    return pl.pallas_call(
        functools.partial(_attn_prompt_kernel, seq=seq),
        grid=(batch, nqb),
        in_specs=[qrow(Q_COLS), srow(4 * KV_COLS), srow(2 * KV_COLS),
                  pl.BlockSpec((GATE_ROWS, Q_BLOCK), lambda b, i: (0, b * nqb + i)),
                  _const_spec(pe4.shape), _const_spec(w4.shape), _const_spec(cosc.shape), _const_spec(sinc.shape)],
        out_specs=qrow(Q_COLS),
        out_shape=jax.ShapeDtypeStruct((t, Q_COLS), F32),
        scratch_shapes=[pltpu.VMEM((2, seq, KV_COLS), F32)]
        + [pltpu.VMEM((KVH, seq, HD), BF16), pltpu.VMEM((KVH, HD, seq), BF16)] * 2
        + [pltpu.VMEM((KVH, N_CMP, HD), BF16)] * 2
        + [pltpu.VMEM((KVH * Q_BLOCK // 128, N_CMP, 128), F32), pltpu.VMEM((KVH, seq // L_SEL, HPG * Q_BLOCK), F32)],
        compiler_params=_params(("parallel", "arbitrary")),
        name="attn_prompt",
    )(q, kv, win, gt, pe4, w4, cosc, sinc)


def _softmax_rows(parts):
    masked = [jnp.where(k, s, NEG_INF) for s, k in parts]
    m = masked[0].max(axis=1, keepdims=True)
    for s in masked[1:]:
        m = jnp.maximum(m, s.max(axis=1, keepdims=True))
    es = [jnp.where(k, jnp.exp2(s - m), 0.0) for s, (_, k) in zip(masked, parts)]
    l = es[0].sum(axis=1, keepdims=True)
    for e in es[1:]:
        l = l + e.sum(axis=1, keepdims=True)
    return es, jnp.maximum(l, TINY)


def _attn_sample_kernel(pt_ref, q_ref, kv_ref, win_ref, gate_ref, sw_ref, *rest, n_pages, page, past):
    del pt_ref
    n_seq = q_ref.shape[0]
    pe4_ref, w4_ref, cosc_ref, sinc_ref, expand_ref, perm_ref = rest[n_seq * n_pages:n_seq * n_pages + 6]
    o_ref, wout_all_ref = rest[-2:]
    n_prev = wout_all_ref.shape[0] - 1
    if n_prev:
        wout_all_ref[0:n_prev] = rest[-3][...]
    wout_ref = wout_all_ref.at[n_prev]
    group = perm_ref.shape[0] // page
    moved = []
    for r in range(n_seq):
        pages = rest[r * n_pages:(r + 1) * n_pages]
        for jj in range(n_pages // group):
            x_t = jnp.concatenate([pages[group * jj + k][0, 0:2 * KV_COLS, :] for k in range(group)], axis=1)
            hi = x_t.astype(BF16)
            lo = (x_t - hi.astype(F32)).astype(BF16)
            moved.append(_dot_nt(perm_ref[...], hi) + _dot_nt(perm_ref[...], lo))
    rows_at = lambda p: jnp.concatenate([m[8 * p:8 * p + 8] for m in moved], axis=0)
    kcmp, vcmp = _compress(rows_at, pe4_ref, w4_ref, cosc_ref, sinc_ref, n_seq * N_CMP)
    for r in range(n_seq):
        one = lambda a: [x[r * N_CMP:(r + 1) * N_CMP] for x in a]
        _attn_sample_seq(r, rest[r * n_pages:(r + 1) * n_pages], one(kcmp), one(vcmp), q_ref, kv_ref, win_ref,
                         gate_ref, sw_ref, expand_ref, o_ref, wout_ref, past)


def _attn_sample_seq(r, pages, kcmp, vcmp, q_ref, kv_ref, win_ref, gate_ref, sw_ref, expand_ref, o_ref, wout_ref,
                     past):
    n_pages = len(pages)
    nq = q_ref.shape[1]
    wb = sw_ref.shape[2]
    new_pad = 16
    cached = lambda r0: jnp.concatenate([pages[j][0, r0:r0 + HD, :].astype(BF16) for j in range(n_pages)], axis=1)
    kt = [cached(2 * KV_COLS + g * HD) for g in range(KVH)]
    vt = [cached(3 * KV_COLS + g * HD) for g in range(KVH)]

    qf = q_ref[r]
    kvn = kv_ref[r]
    wn = win_ref[r]
    gates = gate_ref[r]
    wout_ref[r, :, 0:wb - nq] = sw_ref[r, :, nq:wb]
    wout_ref[r, :, wb - nq:wb] = wn.T

    groups = range(KVH)
    rows = H_A * nq
    grp = lambda a, g: a[g * HPG * nq:(g + 1) * HPG * nq]
    per_group = lambda f: jnp.concatenate([f(g) for g in groups], axis=0)
    zpad = jnp.zeros((new_pad - nq, HD), F32)
    new_rows = lambda a: jnp.concatenate([a, zpad], axis=0).astype(BF16)
    ns_real = past // L_SEL + 1
    iq = lambda n: lax.broadcasted_iota(jnp.int32, (rows, n), 0) % nq
    ik = lambda n: lax.broadcasted_iota(jnp.int32, (rows, n), 1)
    causal_new = (ik(new_pad) <= iq(new_pad)) & (ik(new_pad) < nq)
    qg = [jnp.concatenate([qf[:, (g * HPG + h) * HD:(g * HPG + h + 1) * HD] for h in range(HPG)],
                          axis=0).astype(BF16) for g in groups]
    m_c = (ik(N_CMP) * L_CMP + (L_CMP - 1)) <= past + iq(N_CMP)
    (e_c,), l_c = _softmax_rows([(per_group(lambda g: _dot_nt(qg[g], kcmp[g])), m_c)])
    p_c = e_c / l_c
    o_c = per_group(lambda g: _dot(grp(p_c, g).astype(BF16), vcmp[g]))
    imp = per_group(lambda g: sum(p_c[(g * HPG + h) * nq:(g * HPG + h + 1) * nq] for h in range(HPG)))
    imp = jnp.concatenate([imp, jnp.zeros((KVH * nq, 128 - N_CMP), F32)], axis=1)
    imp2 = imp + pltpu.roll(imp, 127, 1)
    lane = lax.broadcasted_iota(jnp.int32, (KVH * nq, 128), 1)
    blk = lane // 2
    t_sel = past + lax.broadcasted_iota(jnp.int32, (KVH * nq, 128), 0) % nq
    cur = t_sel // L_SEL
    forced = (blk == 0) | (blk == cur) | (blk == cur - 1)
    valid = (blk * L_SEL) <= t_sel
    score = jnp.where(valid, imp2 + jnp.where(forced, FORCE_BONUS, 0.0), -FORCE_BONUS)
    score = jnp.where(((lane & 1) == 0) & (blk < ns_real), score, -jnp.inf)
    cnt = jnp.zeros((KVH * nq, 128), jnp.int32)
    for k in range(ns_real):
        ck = score[:, 2 * k:2 * k + 1]
        beats = (ck > score) | ((ck == score) & (lane > 2 * k))
        cnt = cnt + beats.astype(jnp.int32)
    sel = (cnt < N_SEL).astype(BF16)
    sel_rows = per_group(lambda g: jnp.concatenate([sel[g * nq:(g + 1) * nq]] * HPG, axis=0))
    picked = _dot(sel_rows, expand_ref[...]) > 0.5
    s_p = per_group(lambda g: _dot(qg[g], kt[g]))
    s_n = per_group(lambda g: _dot_nt(qg[g], new_rows(kvn[:, 2 * KV_COLS + g * HD:2 * KV_COLS + (g + 1) * HD])))
    (e_p, e_n), l_s = _softmax_rows([(s_p, picked[:, 0:past]), (s_n, picked[:, past:past + new_pad] & causal_new)])
    o_s = per_group(lambda g: _dot_nt(grp(e_p, g).astype(BF16), vt[g]) + _dot(
        grp(e_n, g).astype(BF16), new_rows(kvn[:, 3 * KV_COLS + g * HD:3 * KV_COLS + (g + 1) * HD]))) / l_s
    s_wp = per_group(lambda g: _dot(qg[g], sw_ref[r, g * HD:(g + 1) * HD, :].astype(BF16)))
    s_wn = per_group(lambda g: _dot_nt(qg[g], new_rows(wn[:, g * HD:(g + 1) * HD])))
    (e_wp, e_wn), l_w = _softmax_rows([(s_wp, ik(wb) > iq(wb) + (wb - WINDOW)), (s_wn, causal_new)])
    o_w = per_group(lambda g: _dot_nt(
        grp(e_wp, g).astype(BF16), sw_ref[r, KV_COLS + g * HD:KV_COLS + (g + 1) * HD, :].astype(BF16)) + _dot(
        grp(e_wn, g).astype(BF16), new_rows(wn[:, KV_COLS + g * HD:KV_COLS + (g + 1) * HD]))) / l_w
    gate = lambda br: jnp.concatenate([gates[:, 3 * hd + br:3 * hd + br + 1] for hd in range(H_A)], axis=0)
    o = gate(0) * o_c + gate(1) * o_s + gate(2) * o_w
    for pair in range(H_A // 2):
        o_ref[r, :, 2 * pair * HD:(2 * pair + 2) * HD] = jnp.concatenate(
            [o[2 * pair * nq:(2 * pair + 1) * nq], o[(2 * pair + 1) * nq:(2 * pair + 2) * nq]], axis=1)


def _attn_sample(page_ids, win_off, q3, kv3, win3, gates3, state_win_t, cache_t, pe4, w4, cosc, sinc, expand, perm,
                 wout_prev):
    nb, nq, _ = q3.shape
    n_prev = 0 if wout_prev is None else wout_prev.shape[0]
    n_pages = page_ids.shape[1]
    page = cache_t.shape[2]
    past = n_pages * page
    wb = state_win_t.shape[2]
    ns = SAMPLE_SEQS
    wout_spec = lambda n: pl.BlockSpec((n, ns, 2 * KV_COLS, wb), lambda b, pt: (0, b, 0, 0))
    seq3 = lambda r, w: pl.BlockSpec((ns, r, w), lambda b, pt: (b, 0, 0))
    page_spec = lambda r, j: pl.BlockSpec((1, 4 * KV_COLS, page), lambda b, pt: (pt[ns * b + r, j], 0, 0))
    const = lambda a: pl.BlockSpec(a.shape, lambda b, pt: (0,) * a.ndim, pipeline_mode=pl.Buffered(1))
    grid_spec = pltpu.PrefetchScalarGridSpec(
        num_scalar_prefetch=1,
        grid=(nb // ns,),
        in_specs=[seq3(nq, Q_COLS), seq3(nq, 4 * KV_COLS), seq3(nq, 2 * KV_COLS), seq3(nq, 128),
                  pl.BlockSpec((ns, 2 * KV_COLS, wb), lambda b, pt: (win_off // ns + b, 0, 0))]
        + [page_spec(r, j) for r in range(ns) for j in range(n_pages)]
        + [const(pe4), const(w4), const(cosc), const(sinc), const(expand), const(perm)]
        + [wout_spec(n_prev)] * bool(n_prev),
        out_specs=[seq3(nq, Q_COLS), wout_spec(n_prev + 1)],
    )
    return pl.pallas_call(
        functools.partial(_attn_sample_kernel, n_pages=n_pages, page=page, past=past),
        grid_spec=grid_spec,
        out_shape=[jax.ShapeDtypeStruct((nb, nq, Q_COLS), F32),
                   jax.ShapeDtypeStruct((n_prev + 1, nb, 2 * KV_COLS, wb), F32)],
        compiler_params=_params(("arbitrary",)),
        name="attn_sample",
    )(page_ids, q3, kv3, win3, gates3, state_win_t, *([cache_t] * (ns * n_pages)), pe4, w4, cosc, sinc, expand, perm,
      *([wout_prev] * bool(n_prev)))


def _conv_kernel(ue_ref, w_ref, b_ref, g_ref, bb_ref, c_ref, shift_scr, *, rows):
    nb, st, ch = c_ref.shape
    pad = CONV_HALO - (CONV_W - 1)
    for r0 in range(0, st, rows):
        acc = jnp.zeros((nb, rows, ch), F32)
        for res in range(8):
            taps = range(res, CONV_W, 8)
            span = 8 * (len(taps) - 1) + rows
            shift_scr[res, :, 0:span, :] = ue_ref[:, pl.ds(pad + r0 + res, span), :]
            for m, k in enumerate(taps):
                acc = acc + shift_scr[res, :, 8 * m:8 * m + rows, :] * w_ref[k:k + 1, :].reshape(1, 1, ch)
        y = _layer_norm(acc + b_ref[...].reshape(1, 1, ch), g_ref[...].reshape(1, 1, ch), bb_ref[...].reshape(1, 1, ch))
        c_ref[:, r0:r0 + rows, :] = y * _sigmoid(y)


def _with_history(body, halo):
    def kernel_fn(cur_ref, prev_ref, *rest):
        ue_scr = rest[-1]
        first = pl.program_id(1) == 0
        ue_scr[:, 0:halo, :] = jnp.where(first, 0.0, prev_ref[...])
        ue_scr[:, halo:, :] = cur_ref[...]
        body(ue_scr, *rest[:-1])
    return kernel_fn


def _conv_shift_scratch(nb, rows, ch):
    return pltpu.VMEM((8, nb, 8 * ((CONV_W - 1) // 8) + rows, ch), F32)


def _halo_specs(ts, halo, ch):
    cur = pl.BlockSpec((1, ts, ch), lambda b, i: (b, i, 0))
    prev = pl.BlockSpec((1, halo, ch), lambda b, i: (b, jnp.maximum(i * (ts // halo) - 1, 0), 0))
    return cur, prev


def _conv_module_seq(u3, w, b, g, bb, ts):
    nseq, s, ch = u3.shape
    cur, prev = _halo_specs(ts, CONV_HALO, ch)
    consts = (w, b, g, bb)
    return pl.pallas_call(
        _with_history(functools.partial(_conv_kernel, rows=32), CONV_HALO),
        grid=(nseq, s // ts),
        in_specs=[cur, prev] + [pl.BlockSpec(a.shape, lambda bi, i: (0, 0), pipeline_mode=pl.Buffered(1)) for a in consts],
        out_specs=cur,
        out_shape=jax.ShapeDtypeStruct((nseq, s, ch), F32),
        scratch_shapes=[_conv_shift_scratch(1, 32, ch), pltpu.VMEM((1, ts + CONV_HALO, ch), F32)],
        compiler_params=_params(("parallel", "arbitrary")),
        name="conv_module_seq",
    )(u3, u3, *consts)


def _conv_module(ue_tiles, w, b, g, bb, nb, st):
    n = ue_tiles.shape[0]
    ch = ue_tiles.shape[2]
    blk = lambda r: pl.BlockSpec((nb, r, ch), lambda i: (i, 0, 0))
    return pl.pallas_call(
        functools.partial(_conv_kernel, rows=min(st, 32)),
        grid=(n // nb,),
        in_specs=[blk(st + CONV_HALO)] + [_const_spec(a.shape) for a in (w, b, g, bb)],
        out_specs=blk(st),
        out_shape=jax.ShapeDtypeStruct((n, st, ch), F32),
        scratch_shapes=[_conv_shift_scratch(nb, min(st, 32), ch)],
        compiler_params=_params(("parallel",)),
        name="conv_module",
    )(ue_tiles, w, b, g, bb)


def _pool_kernel(ue_ref, d_ref, *, rows, tile_axis, pos0):
    nb, st, _ = d_ref.shape
    tile = 0 if tile_axis is None else pl.program_id(tile_axis)
    for r0 in range(0, st, rows):
        pos = pos0 + tile * st + r0 + lax.broadcasted_iota(jnp.int32, (1, rows, G_C), 1)
        for g, wdw in enumerate(POOL_WINDOWS):
            cols = slice(g * G_C, (g + 1) * G_C)
            acc = ue_ref[:, pl.ds(POOL_HALO + r0, rows), cols]
            u = acc
            for k in range(1, wdw):
                acc = acc + ue_ref[:, pl.ds(POOL_HALO + r0 - k, rows), cols]
            cnt = jnp.minimum(pos + 1, wdw).astype(F32)
            d_ref[:, r0:r0 + rows, cols] = acc / cnt - u


def _pool_diff_seq(u3, ts):
    nseq, s, ch = u3.shape
    cur, prev = _halo_specs(ts, POOL_HALO, ch)
    return pl.pallas_call(
        _with_history(functools.partial(_pool_kernel, rows=64, tile_axis=1, pos0=0), POOL_HALO),
        grid=(nseq, s // ts),
        in_specs=[cur, prev],
        out_specs=cur,
        out_shape=jax.ShapeDtypeStruct((nseq, s, ch), F32),
        scratch_shapes=[pltpu.VMEM((1, ts + POOL_HALO, ch), F32)],
        compiler_params=_params(("parallel", "arbitrary")),
        name="pool_diff_seq",
    )(u3, u3)


def _pool_diff(ue_tiles, nb, st, pos0):
    n = ue_tiles.shape[0]
    ch = ue_tiles.shape[2]
    blk = lambda r: pl.BlockSpec((nb, r, ch), lambda i: (i, 0, 0))
    return pl.pallas_call(
        functools.partial(_pool_kernel, rows=min(st, 64), tile_axis=None, pos0=pos0),
        grid=(n // nb,),
        in_specs=[blk(st + POOL_HALO)],
        out_specs=blk(st),
        out_shape=jax.ShapeDtypeStruct((n, st, ch), F32),
        compiler_params=_params(("parallel",)),
        name="pool_diff",
    )(ue_tiles)


def _ffn_ln(x1, w1_ref, w3_ref, w2_ref, g2_ref, b2_ref):
    xb = x1.astype(BF16)
    h1 = _dot(xb, w1_ref[...])
    h3 = _dot(xb, w3_ref[...])
    gated = (h1 * _sigmoid(h1) * h3).astype(BF16)
    return _layer_norm(ALPHA * x1 + _dot(gated, w2_ref[...]), g2_ref[...], b2_ref[...])


def _post_a_kernel(x_ref, oa_ref, c_ref, wo1_ref, wo2_ref, g1_ref, b1_ref, w1_ref, w3_ref, w2_ref, g2_ref, b2_ref,
                   wnext_ref, o_ref, unext_ref):
    y = _dot(oa_ref[...].astype(BF16), wo1_ref[...]) + _dot(c_ref[...].astype(BF16), wo2_ref[...])
    x1 = _layer_norm(ALPHA * x_ref[...] + y, g1_ref[...], b1_ref[...])
    x2 = _ffn_ln(x1, w1_ref, w3_ref, w2_ref, g2_ref, b2_ref)
    o_ref[...] = x2
    unext_ref[...] = _dot(x2.astype(BF16), wnext_ref[...])


def _post_c_kernel(x_ref, d_ref, wg_ref, sc_ref, wo_ref, g1_ref, b1_ref, w1_ref, w3_ref, w2_ref, g2_ref, b2_ref,
                   o_ref):
    db = d_ref[...].astype(BF16)
    z = jnp.concatenate([_dot(db[:, g * G_C:(g + 1) * G_C], wg_ref[g]) for g in range(len(POOL_WINDOWS))], axis=1)
    y = _dot((z * sc_ref[...]).astype(BF16), wo_ref[...])
    x1 = _layer_norm(ALPHA * x_ref[...] + y, g1_ref[...], b1_ref[...])
    o_ref[...] = _ffn_ln(x1, w1_ref, w3_ref, w2_ref, g2_ref, b2_ref)


def _post(body, x, acts, consts, tm, n_out=1):
    t = x.shape[0]
    row = lambda a: pl.BlockSpec((tm, a.shape[1]), lambda i: (i, 0))
    outs = pl.pallas_call(
        body,
        grid=(t // tm,),
        in_specs=[row(x)] + [row(a) for a in acts] + [_const_spec(c.shape) for c in consts],
        out_specs=[row(x)] * n_out,
        out_shape=[jax.ShapeDtypeStruct(x.shape, F32)] * n_out,
        compiler_params=_params(("parallel",)),
        name=body.__name__.strip("_"),
    )(x, *acts, *consts)
    return outs[0] if n_out == 1 else outs


def _rope_tables(pos):
    half = HD // 2
    inv = ROPE_THETA ** (-jnp.arange(half, dtype=F32) / half)
    ang = pos.astype(F32)[:, None] * inv[None, :]
    cos, sin = jnp.cos(ang), jnp.sin(ang)
    return jnp.tile(cos, (1, 4)), jnp.tile(jnp.concatenate([-sin, sin], axis=1), (1, 2))


def _with_hist_rows(u3, hist, halo):
    b, _, c = u3.shape
    return jnp.concatenate([jnp.zeros((b, halo - hist.shape[1], c), u3.dtype), hist, u3], axis=1)


def _prep_weights(p):
    n_a = p["w_in_a"].shape[0]
    cuts = np.cumsum([0, Q_COLS, 4 * KV_COLS, 2 * KV_COLS, GATE_COLS, C_B, C_B])
    out = {"a": [], "c": [], "ffn": []}
    for ia in range(n_a):
        w_in = p["w_in_a"][ia]
        wq, wkv, wwin, wgl, wga, wgb = [w_in[:, cuts[k]:cuts[k + 1]].astype(BF16) for k in range(6)]
        wglt = jnp.pad(wgl.T, ((0, GATE_ROWS - GATE_COLS), (0, 0)))
        wglr = jnp.pad(wgl, ((0, 0), (0, 128 - GATE_COLS)))
        wk, wv = p["w_cmp_k"][ia], p["w_cmp_v"][ia]
        blocks = jnp.stack([wk, wk, wv, wv], axis=1)
        w4 = jnp.einsum("pkde,kl->pkdle", blocks, jnp.eye(4, dtype=F32)).reshape(L_CMP, 4 * HD, 4 * HD).astype(BF16)
        pe4 = jnp.concatenate([p["pe_cmp_k"][ia]] * 2 + [p["pe_cmp_v"][ia]] * 2, axis=1)
        wo = p["w_out_a"][ia].astype(BF16)
        out["a"].append(dict(
            inproj_t=(wq, wkv, wwin, wga, wgb, wglt), inproj_r=(wq, wkv, wwin, wga, wgb, wglr), w4=w4, pe4=pe4,
            conv=(jnp.pad(p["conv_w"][ia], ((0, 1), (0, 0))), p["conv_b"][ia][None], p["conv_ln_g"][ia][None],
                  p["conv_ln_b"][ia][None]),
            wo1=wo[:Q_COLS], wo2=wo[Q_COLS:]))
    for ic in range(p["w_in_c"].shape[0]):
        out["c"].append(dict(w_in=p["w_in_c"][ic].astype(BF16), w_grp=p["w_grp_c"][ic].astype(BF16),
                             scale=p["scale_c"][ic][None], w_out=p["w_out_c"][ic].astype(BF16)))
    for l in range(DEPTH):
        out["ffn"].append((p["ln1_g"][l][None], p["ln1_b"][l][None], p["w1"][l].astype(BF16),
                           p["w3"][l].astype(BF16), p["w2"][l].astype(BF16), p["ln2_g"][l][None], p["ln2_b"][l][None]))
    return out


def _trunk(x3, pos0, wb, wts, cache_kv, page_table, state_win, state_conv, state_pool):
    b, s, _ = x3.shape
    t = b * s
    prompt = cache_kv is None
    tm = min(512, t)
    x = x3.reshape(t, D_MODEL)
    cos, sin = _rope_tables(pos0 + jnp.arange(s, dtype=jnp.int32))
    if not prompt:
        cos, sin = jnp.tile(cos, (tm // s, 1)), jnp.tile(sin, (tm // s, 1))
    cosc, sinc = _rope_tables(jnp.arange(N_CMP, dtype=jnp.int32) * L_CMP + (L_CMP - 1))
    ts = 512 if prompt else s
    nb = 1 if prompt else 8
    if not prompt:
        n_a, n_pool, page = cache_kv.shape[:3]
        cache_t = jnp.transpose(cache_kv, (0, 1, 3, 4, 5, 2)).reshape(n_a * n_pool, 4 * KV_COLS, page)
        sw_t = jnp.transpose(state_win, (0, 1, 3, 4, 5, 2)).reshape(n_a * b, 2 * KV_COLS, wb)
        lane_blk = jnp.arange(128)[:, None]
        key_blk = jnp.arange(page_table.shape[1] * page + 128)[None, :] // L_SEL
        expand = ((lane_blk % 2 == 0) & (lane_blk // 2 == key_blk)).astype(BF16)
        out_row = jnp.arange(8 * L_CMP)[:, None]
        position = jnp.arange(8 * L_CMP)[None, :]
        perm = ((out_row // 8 == position % L_CMP) & (out_row % 8 == position // L_CMP)).astype(BF16)
    kvt_all, wout_all, win_new, conv_new, pool_new = None, None, [], [], []
    for l in range(DEPTH):
        ln1_g, ln1_b, w1, w3, w2, ln2_g, ln2_b = wts["ffn"][l]
        if l % 2 == 0:
            a = wts["a"][l // 2]
            q, kv, win, u, gates, kvt_all = _inproj_a(x, a["inproj_t" if prompt else "inproj_r"], cos, sin, tm,
                                                      prompt, kvt_all, s)
            if prompt:
                oa = _attn_prompt(q, kv, win, gates, a["pe4"], a["w4"], cosc, sinc, b, s)
                win_new.append(win.reshape(b, s, 2, KVH, HD)[:, s - wb:])
                hist = None
            else:
                ia = l // 2
                n_pool = cache_kv.shape[1]
                oa, wout_all = _attn_sample(page_table + ia * n_pool, ia * b, q.reshape(b, s, Q_COLS),
                                            kv.reshape(b, s, 4 * KV_COLS), win.reshape(b, s, 2 * KV_COLS),
                                            gates.reshape(b, s, 128), sw_t, cache_t, a["pe4"], a["w4"], cosc, sinc,
                                            expand, perm, wout_all)
                oa = oa.reshape(t, Q_COLS)
                hist = state_conv[ia]
            u3 = u.reshape(b, s, C_B)
            keep = CONV_W - 1
            conv_new.append(u3[:, s - keep:] if hist is None else jnp.concatenate([hist, u3], axis=1)[:, -keep:])
            if prompt:
                c = _conv_module_seq(u3, *a["conv"], ts).reshape(t, C_B)
            else:
                c = _conv_module(_with_hist_rows(u3, hist, CONV_HALO), *a["conv"], nb, s).reshape(t, C_B)
            x, u_next = _post(_post_a_kernel, x, (oa, c), (a["wo1"], a["wo2"], ln1_g, ln1_b, w1, w3, w2, ln2_g, ln2_b,
                                                           wts["c"][l // 2]["w_in"]), tm, n_out=2)
        else:
            cw = wts["c"][l // 2]
            u3 = u_next.reshape(b, s, D_MODEL)
            hist = None if prompt else state_pool[l // 2]
            if prompt:
                d = _pool_diff_seq(u3, ts).reshape(t, D_MODEL)
            else:
                d = _pool_diff(_with_hist_rows(u3, hist, POOL_HALO), nb, s, pos0).reshape(t, D_MODEL)
            pool_new.append(u3[:, s - POOL_HIST:] if hist is None
                            else jnp.concatenate([hist, u3], axis=1)[:, -POOL_HIST:])
            x = _post(_post_c_kernel, x, (d,),
                      (cw["w_grp"], cw["scale"], cw["w_out"], ln1_g, ln1_b, w1, w3, w2, ln2_g, ln2_b), tm)
    kv_out = jnp.moveaxis(kvt_all, 2, 3).reshape(-1, b, s, 4, KVH, HD)
    if prompt:
        win_out = jnp.stack(win_new)
    else:
        win_out = wout_all.reshape(-1, b, 2, KVH, HD, wb).transpose(0, 1, 5, 2, 3, 4)
    return x.reshape(b, s, D_MODEL), kv_out, win_out, jnp.stack(conv_new), jnp.stack(pool_new)


def kernel(x_prompt, x_sample, cache_kv, state_win, state_conv, state_pool, page_table, w_in_a, w_cmp_k, pe_cmp_k, w_cmp_v, pe_cmp_v, conv_w, conv_b, conv_ln_g, conv_ln_b, w_out_a, w_in_c, w_grp_c, scale_c, w_out_c, ln1_g, ln1_b, ln2_g, ln2_b, w1, w3, w2):
    p = dict(w_in_a=w_in_a, w_cmp_k=w_cmp_k, pe_cmp_k=pe_cmp_k, w_cmp_v=w_cmp_v, pe_cmp_v=pe_cmp_v,
             conv_w=conv_w, conv_b=conv_b, conv_ln_g=conv_ln_g, conv_ln_b=conv_ln_b, w_out_a=w_out_a,
             w_in_c=w_in_c, w_grp_c=w_grp_c, scale_c=scale_c, w_out_c=w_out_c,
             ln1_g=ln1_g, ln1_b=ln1_b, ln2_g=ln2_g, ln2_b=ln2_b, w1=w1, w3=w3, w2=w2)
    wts = _prep_weights(p)
    past_len = page_table.shape[1] * cache_kv.shape[2]
    wb = state_win.shape[2]
    y_p, kv_p, win_p, conv_p, pool_p = _trunk(x_prompt, 0, wb, wts, None, None, None, None, None)
    y_s, kv_s, win_s, conv_s, pool_s = _trunk(x_sample, past_len, wb, wts, cache_kv, page_table, state_win,
                                              state_conv, state_pool)
    return (y_p, y_s, kv_p, kv_s, win_p, win_s, conv_p, conv_s, pool_p, pool_s)
```

```python
import functools

import numpy as np
import jax
import jax.numpy as jnp
from jax import lax
from jax.experimental import pallas as pl
from jax.experimental.pallas import tpu as pltpu

F32 = jnp.float32
BF16 = jnp.bfloat16

D_MODEL = 1024
DEPTH = 4
HD = 64
H_A = 8
KVH = 2
HPG = H_A // KVH
L_CMP = 32
L_SEL = 64
N_SEL = 8
WINDOW = 512
ROPE_THETA = 10000.0
FORCE_BONUS = 1.0e4
NEG_INF = -1.0e30
TINY = 1.0e-30
C_B = D_MODEL // 2
CONV_W = 31
POOL_WINDOWS = (2, 4, 8, 16)
G_C = D_MODEL // len(POOL_WINDOWS)
POOL_HIST = max(POOL_WINDOWS) - 1
D_FF = 2816
ALPHA = (2 * DEPTH) ** 0.25
Q_COLS = H_A * HD
KV_COLS = KVH * HD
GATE_COLS = 3 * H_A
GATE_ROWS = 32
Q_BLOCK = 512
N_CMP = 64
SEL_CHUNK = Q_BLOCK
WIN_SPAN = WINDOW + Q_BLOCK
SAMPLE_SEQS = 2
CONV_HALO = 32
POOL_HALO = 16
VMEM_LIMIT = 56 * 1024 * 1024
LOG2E = 1.4426950408889634
Q_SCALE = HD ** -0.5 * LOG2E


def _dot(a, b):
    return jnp.dot(a, b, preferred_element_type=F32)


def _dot_nt(a, b):
    return lax.dot_general(a, b, (((1,), (1,)), ((), ())), preferred_element_type=F32)


def _dot_tn(a, b):
    return lax.dot_general(a, b, (((0,), (0,)), ((), ())), preferred_element_type=F32)


def _sigmoid(x):
    return 1.0 / (1.0 + jnp.exp(-x))


def _layer_norm(y, g, b):
    mu = jnp.mean(y, axis=-1, keepdims=True)
    d = y - mu
    var = jnp.mean(d * d, axis=-1, keepdims=True)
    return d * lax.rsqrt(var + 1e-5) * g + b


def _rope128(v, cos, sin_signed):
    lane = lax.broadcasted_iota(jnp.int32, v.shape, 1)
    rot = jnp.where((lane % HD) < HD // 2, pltpu.roll(v, 128 - HD // 2, 1), pltpu.roll(v, HD // 2, 1))
    return v * cos + rot * sin_signed


def _params(sem=None):
    return pltpu.CompilerParams(dimension_semantics=sem, vmem_limit_bytes=VMEM_LIMIT)


def _const_spec(shape):
    nd = len(shape)
    return pl.BlockSpec(shape, lambda *_: (0,) * nd, pipeline_mode=pl.Buffered(1))


def _inproj_a_kernel(x_ref, wq_ref, wkv_ref, wwin_ref, wga_ref, wgb_ref, wgl_ref, cos_ref, sin_ref, *rest,
                     gates_transposed, n_prev):
    q_out, kv_out, win_out, u_out, g_out, kvt_all_out = rest[-6:]
    if n_prev:
        kvt_all_out[0:n_prev] = rest[0][...]
    xb = x_ref[...].astype(BF16)
    cos = cos_ref[...]
    sin = sin_ref[...]
    q = _dot(xb, wq_ref[...])
    for c in range(Q_COLS // 128):
        q_out[:, c * 128:(c + 1) * 128] = _rope128(q[:, c * 128:(c + 1) * 128], cos, sin) * Q_SCALE
    kv = _dot(xb, wkv_ref[...])
    kv_out[:, 0:256] = kv[:, 0:256]
    kv_out[:, 256:384] = _rope128(kv[:, 256:384], cos, sin)
    kv_out[:, 384:512] = kv[:, 384:512]
    kvt_all_out[n_prev, 0] = kv_out[...].T
    w = _dot(xb, wwin_ref[...])
    win_out[:, 0:128] = _rope128(w[:, 0:128], cos, sin)
    win_out[:, 128:256] = w[:, 128:256]
    u_out[...] = _dot(xb, wga_ref[...]) * _sigmoid(_dot(xb, wgb_ref[...]))
    g_out[...] = _sigmoid(_dot_nt(wgl_ref[...], xb) if gates_transposed else _dot(xb, wgl_ref[...]))


def _inproj_a(x, wts, cos, sin, tm, gates_transposed, kvt_prev, seq):
    t = x.shape[0]
    n_prev = 0 if kvt_prev is None else kvt_prev.shape[0]
    nt = max(seq // tm, 1)
    kvt_spec = lambda n: pl.BlockSpec((n, 1, 4 * KV_COLS, tm), lambda i: (0, i // nt, 0, i % nt))
    kvt_cols = max(seq, tm)
    g_spec = pl.BlockSpec((GATE_ROWS, tm), lambda i: (0, i)) if gates_transposed else pl.BlockSpec((tm, 128), lambda i: (i, 0))
    g_shape = (GATE_ROWS, t) if gates_transposed else (t, 128)
    n_tab = cos.shape[0] // tm
    row = lambda w: pl.BlockSpec((tm, w), lambda i: (i, 0))
    tab = pl.BlockSpec((tm, 128), lambda i: (i % n_tab, 0))
    return pl.pallas_call(
        functools.partial(_inproj_a_kernel, gates_transposed=gates_transposed, n_prev=n_prev),
        grid=(t // tm,),
        in_specs=[row(D_MODEL)] + [_const_spec(w.shape) for w in wts] + [tab, tab] + [kvt_spec(n_prev)] * bool(n_prev),
        out_specs=[row(Q_COLS), row(4 * KV_COLS), row(2 * KV_COLS), row(C_B), g_spec, kvt_spec(n_prev + 1)],
        out_shape=[jax.ShapeDtypeStruct((t, Q_COLS), F32), jax.ShapeDtypeStruct((t, 4 * KV_COLS), F32),
                   jax.ShapeDtypeStruct((t, 2 * KV_COLS), F32), jax.ShapeDtypeStruct((t, C_B), F32),
                   jax.ShapeDtypeStruct(g_shape, F32),
                   jax.ShapeDtypeStruct((n_prev + 1, t // kvt_cols, 4 * KV_COLS, kvt_cols), F32)],
        compiler_params=_params(("parallel",)),
        name="inproj_a",
    )(x, *wts, cos, sin, *([kvt_prev] * bool(n_prev)))


def _compress(rows_at, pe4_ref, w4_ref, cosc_ref, sinc_ref, n_blocks=N_CMP):
    acc = jnp.zeros((n_blocks, 4 * HD), F32)
    for p in range(L_CMP):
        xp = rows_at(p) + pe4_ref[p:p + 1, :]
        acc = acc + _dot(xp.astype(BF16), w4_ref[p])
    reps = n_blocks // N_CMP
    k = _rope128(acc[:, 0:2 * HD], jnp.concatenate([cosc_ref[...]] * reps, axis=0),
                 jnp.concatenate([sinc_ref[...]] * reps, axis=0))
    v = acc[:, 2 * HD:4 * HD]
    return ([k[:, g * HD:(g + 1) * HD].astype(BF16) for g in range(KVH)],
            [v[:, g * HD:(g + 1) * HD].astype(BF16) for g in range(KVH)])


def _softmax_t(s, msk):
    s = jnp.where(msk, s, NEG_INF)
    m = jnp.max(s, axis=0, keepdims=True)
    p = jnp.where(msk, jnp.exp2(s - m), 0.0)
    return m, p, jnp.sum(p, axis=0, keepdims=True)


def _nsa_core_t(qf, gt, t_q, t0, kcmp, vcmp, ksel, vselt, kwin, vwint, imp_scr, bias_scr, n_sel_blocks):
    nq = Q_BLOCK
    w = HPG * nq
    groups = range(KVH)
    t_lane = jnp.concatenate([t_q] * HPG, axis=1)
    row_c = lax.broadcasted_iota(jnp.int32, (N_CMP, w), 0)
    key_row = lax.broadcasted_iota(jnp.int32, (nq, nq), 0)
    heads = lambda a: jnp.concatenate([a] * HPG, axis=1)
    sel_per_chunk = SEL_CHUNK // L_SEL
    n_full = t0 // SEL_CHUNK
    win_lo = pl.multiple_of(jnp.maximum(t0 + nq - WIN_SPAN, 0), nq)
    causal_bias = heads(jnp.where(t0 + key_row <= t_q, 0.0, NEG_INF))
    win_bias = []
    for k in range(WIN_SPAN // nq):
        kpos = win_lo + k * nq + key_row
        ok = kpos <= t_q
        if k == 0:
            ok = ok & (kpos > t_q - WINDOW)
        win_bias.append(heads(jnp.where(ok, 0.0, NEG_INF)))
    qgs = [jnp.concatenate([qf[:, (g * HPG + h) * HD:(g * HPG + h + 1) * HD] for h in range(HPG)],
                           axis=0).astype(BF16) for g in groups]
    o_c = []
    for g in groups:
        m_c = (row_c * L_CMP + (L_CMP - 1)) <= t_lane
        _, e_c, l_c = _softmax_t(_dot_nt(kcmp[g], qgs[g]), m_c)
        p_c = e_c / jnp.maximum(l_c, TINY)
        o_c.append(_dot_tn(vcmp[g], p_c.astype(BF16)))
        imp = p_c[:, 0:nq]
        for h in range(1, HPG):
            imp = imp + p_c[:, h * nq:(h + 1) * nq]
        half = N_CMP // 2
        pair_sums = []
        for c in range(nq // 128):
            slot = g * (nq // 128) + c
            imp_scr[slot] = imp[:, c * 128:(c + 1) * 128]
            pair_sums.append(imp_scr[slot, pl.ds(0, half, stride=2), :] + imp_scr[slot, pl.ds(1, half, stride=2), :])
        imp2 = jnp.concatenate(pair_sums, axis=1)
        blk = lax.broadcasted_iota(jnp.int32, (n_sel_blocks, nq), 0)
        cur = t_q // L_SEL
        forced = (blk == 0) | (blk == cur) | (blk == cur - 1)
        valid = (blk * L_SEL) <= t_q
        score = jnp.where(valid, imp2 + jnp.where(forced, FORCE_BONUS, 0.0), -FORCE_BONUS)
        cnt = jnp.zeros((n_sel_blocks, nq), jnp.int32)
        for k in range(n_sel_blocks):
            rk = score[k:k + 1, :]
            beats = (rk > score) | ((rk == score) & (blk > k))
            cnt = cnt + beats.astype(jnp.int32)
        bias_scr[g] = jnp.concatenate([jnp.where(cnt < N_SEL, 0.0, NEG_INF)] * HPG, axis=1)

    def sel_chunk(j, carries, diagonal):
        off = pl.multiple_of(j * SEL_CHUNK, SEL_CHUNK)
        out = []
        for g in groups:
            m, l, acc = carries[g]
            s = _dot_nt(ksel[g, pl.ds(off, SEL_CHUNK), :], qgs[g])
            s = jnp.concatenate([s[r * L_SEL:(r + 1) * L_SEL] + bias_scr[g, pl.ds(j * sel_per_chunk + r, 1), :]
                                 for r in range(sel_per_chunk)], axis=0)
            if diagonal:
                s = s + causal_bias
            m_j = jnp.max(s, axis=0, keepdims=True)
            p = jnp.exp2(s - m_j)
            m_new = jnp.maximum(m, m_j)
            a_old = jnp.exp2(m - m_new)
            a_new = jnp.exp2(m_j - m_new)
            pv = _dot(vselt[g, :, pl.ds(off, SEL_CHUNK)], p.astype(BF16))
            out.append((m_new, a_old * l + a_new * jnp.sum(p, axis=0, keepdims=True), a_old * acc + a_new * pv))
        return tuple(out)

    init = tuple((jnp.full((1, w), NEG_INF, F32), jnp.zeros((1, w), F32), jnp.zeros((HD, w), F32)) for _ in groups)
    carries = lax.fori_loop(0, n_full, functools.partial(sel_chunk, diagonal=False), init)
    carries = sel_chunk(n_full, carries, diagonal=True)
    outs = []
    for g in groups:
        _, l_s, a_s = carries[g]
        o_s = a_s / l_s
        s_w = _dot_nt(kwin[g, pl.ds(win_lo, WIN_SPAN), :], qgs[g])
        s_w = jnp.concatenate([s_w[k * nq:(k + 1) * nq] + win_bias[k] for k in range(WIN_SPAN // nq)], axis=0)
        p_w = jnp.exp2(s_w - jnp.max(s_w, axis=0, keepdims=True))
        o_w = (_dot(vwint[g, :, pl.ds(win_lo, WIN_SPAN)], p_w.astype(BF16))
               / jnp.sum(p_w, axis=0, keepdims=True))
        for h in range(HPG):
            r = 3 * (g * HPG + h)
            sl = slice(h * nq, (h + 1) * nq)
            outs.append(gt[r:r + 1, :] * o_c[g][:, sl] + gt[r + 1:r + 2, :] * o_s[:, sl]
                        + gt[r + 2:r + 3, :] * o_w[:, sl])
    return outs


def _attn_prompt_kernel(q_ref, kv_ref, win_ref, gt_ref, pe4_ref, w4_ref, cosc_ref, sinc_ref, o_ref,
                        kcf, ksel, vselt, kwin, vwint, kcmp, vcmp, imp_scr, bias_scr, *, seq):
    i = pl.program_id(1)

    @pl.when(i == 0)
    def _():
        v_t = kv_ref[:, 3 * KV_COLS:4 * KV_COLS].T.astype(BF16)
        vw_t = win_ref[:, KV_COLS:2 * KV_COLS].T.astype(BF16)
        for g in range(KVH):
            ksel[g] = kv_ref[:, 2 * KV_COLS + g * HD:2 * KV_COLS + (g + 1) * HD].astype(BF16)
            kwin[g] = win_ref[:, g * HD:(g + 1) * HD].astype(BF16)
            vselt[g] = v_t[g * HD:(g + 1) * HD]
            vwint[g] = vw_t[g * HD:(g + 1) * HD]
        for c in range(2):
            kcf[c] = kv_ref[:, c * KV_COLS:(c + 1) * KV_COLS]
        rows_at = lambda p: jnp.concatenate([kcf[c, pl.ds(p, N_CMP, stride=L_CMP), :] for c in range(2)], axis=1)
        k_c, v_c = _compress(rows_at, pe4_ref, w4_ref, cosc_ref, sinc_ref)
        for g in range(KVH):
            kcmp[g] = k_c[g]
            vcmp[g] = v_c[g]

    t0 = i * Q_BLOCK
    t_q = t0 + lax.broadcasted_iota(jnp.int32, (1, Q_BLOCK), 1)
    outs = _nsa_core_t(q_ref[...], gt_ref[...], t_q, t0, kcmp, vcmp, ksel, vselt, kwin, vwint, imp_scr, bias_scr,
                       n_sel_blocks=seq // L_SEL)
    for pair in range(H_A // 2):
        o_ref[:, pair * 128:(pair + 1) * 128] = jnp.concatenate(outs[2 * pair:2 * pair + 2], axis=0).T


def _attn_prompt(q, kv, win, gt, pe4, w4, cosc, sinc, batch, seq):
    t = q.shape[0]
    nqb = seq // Q_BLOCK
    qrow = lambda w: pl.BlockSpec((Q_BLOCK, w), lambda b, i: (b * nqb + i, 0))
    srow = lambda w: pl.BlockSpec((seq, w), lambda b, i: (b, 0))
    return pl.pallas_call(
        functools.partial(_attn_prompt_kernel, seq=seq),
        grid=(batch, nqb),
        in_specs=[qrow(Q_COLS), srow(4 * KV_COLS), srow(2 * KV_COLS),
                  pl.BlockSpec((GATE_ROWS, Q_BLOCK), lambda b, i: (0, b * nqb + i)),
                  _const_spec(pe4.shape), _const_spec(w4.shape), _const_spec(cosc.shape), _const_spec(sinc.shape)],
        out_specs=qrow(Q_COLS),
        out_shape=jax.ShapeDtypeStruct((t, Q_COLS), F32),
        scratch_shapes=[pltpu.VMEM((2, seq, KV_COLS), F32)]
        + [pltpu.VMEM((KVH, seq, HD), BF16), pltpu.VMEM((KVH, HD, seq), BF16)] * 2
        + [pltpu.VMEM((KVH, N_CMP, HD), BF16)] * 2
        + [pltpu.VMEM((KVH * Q_BLOCK // 128, N_CMP, 128), F32), pltpu.VMEM((KVH, seq // L_SEL, HPG * Q_BLOCK), F32)],
        compiler_params=_params(("parallel", "arbitrary")),
        name="attn_prompt",
    )(q, kv, win, gt, pe4, w4, cosc, sinc)


def _softmax_rows(parts):
    masked = [jnp.where(k, s, NEG_INF) for s, k in parts]
    m = masked[0].max(axis=1, keepdims=True)
    for s in masked[1:]:
        m = jnp.maximum(m, s.max(axis=1, keepdims=True))
    es = [jnp.where(k, jnp.exp2(s - m), 0.0) for s, (_, k) in zip(masked, parts)]
    l = es[0].sum(axis=1, keepdims=True)
    for e in es[1:]:
        l = l + e.sum(axis=1, keepdims=True)
    return es, jnp.maximum(l, TINY)


def _attn_sample_kernel(pt_ref, q_ref, kv_ref, win_ref, gate_ref, sw_ref, *rest, n_pages, page, past):
    del pt_ref
    n_seq = q_ref.shape[0]
    pe4_ref, w4_ref, cosc_ref, sinc_ref, expand_ref, perm_ref = rest[n_seq * n_pages:n_seq * n_pages + 6]
    o_ref, wout_all_ref = rest[-2:]
    n_prev = wout_all_ref.shape[0] - 1
    if n_prev:
        wout_all_ref[0:n_prev] = rest[-3][...]
    wout_ref = wout_all_ref.at[n_prev]
    group = perm_ref.shape[0] // page
    moved = []
    for r in range(n_seq):
        pages = rest[r * n_pages:(r + 1) * n_pages]
        for jj in range(n_pages // group):
            x_t = jnp.concatenate([pages[group * jj + k][0, 0:2 * KV_COLS, :] for k in range(group)], axis=1)
            hi = x_t.astype(BF16)
            lo = (x_t - hi.astype(F32)).astype(BF16)
            moved.append(_dot_nt(perm_ref[...], hi) + _dot_nt(perm_ref[...], lo))
    rows_at = lambda p: jnp.concatenate([m[8 * p:8 * p + 8] for m in moved], axis=0)
    kcmp, vcmp = _compress(rows_at, pe4_ref, w4_ref, cosc_ref, sinc_ref, n_seq * N_CMP)
    for r in range(n_seq):
        one = lambda a: [x[r * N_CMP:(r + 1) * N_CMP] for x in a]
        _attn_sample_seq(r, rest[r * n_pages:(r + 1) * n_pages], one(kcmp), one(vcmp), q_ref, kv_ref, win_ref,
                         gate_ref, sw_ref, expand_ref, o_ref, wout_ref, past)


def _attn_sample_seq(r, pages, kcmp, vcmp, q_ref, kv_ref, win_ref, gate_ref, sw_ref, expand_ref, o_ref, wout_ref,
                     past):
    n_pages = len(pages)
    nq = q_ref.shape[1]
    wb = sw_ref.shape[2]
    new_pad = 16
    cached = lambda r0: jnp.concatenate([pages[j][0, r0:r0 + HD, :].astype(BF16) for j in range(n_pages)], axis=1)
    kt = [cached(2 * KV_COLS + g * HD) for g in range(KVH)]
    vt = [cached(3 * KV_COLS + g * HD) for g in range(KVH)]

    qf = q_ref[r]
    kvn = kv_ref[r]
    wn = win_ref[r]
    gates = gate_ref[r]
    wout_ref[r, :, 0:wb - nq] = sw_ref[r, :, nq:wb]
    wout_ref[r, :, wb - nq:wb] = wn.T

    groups = range(KVH)
    rows = H_A * nq
    grp = lambda a, g: a[g * HPG * nq:(g + 1) * HPG * nq]
    per_group = lambda f: jnp.concatenate([f(g) for g in groups], axis=0)
    zpad = jnp.zeros((new_pad - nq, HD), F32)
    new_rows = lambda a: jnp.concatenate([a, zpad], axis=0).astype(BF16)
    ns_real = past // L_SEL + 1
    iq = lambda n: lax.broadcasted_iota(jnp.int32, (rows, n), 0) % nq
    ik = lambda n: lax.broadcasted_iota(jnp.int32, (rows, n), 1)
    causal_new = (ik(new_pad) <= iq(new_pad)) & (ik(new_pad) < nq)
    qg = [jnp.concatenate([qf[:, (g * HPG + h) * HD:(g * HPG + h + 1) * HD] for h in range(HPG)],
                          axis=0).astype(BF16) for g in groups]
    m_c = (ik(N_CMP) * L_CMP + (L_CMP - 1)) <= past + iq(N_CMP)
    (e_c,), l_c = _softmax_rows([(per_group(lambda g: _dot_nt(qg[g], kcmp[g])), m_c)])
    p_c = e_c / l_c
    o_c = per_group(lambda g: _dot(grp(p_c, g).astype(BF16), vcmp[g]))
    imp = per_group(lambda g: sum(p_c[(g * HPG + h) * nq:(g * HPG + h + 1) * nq] for h in range(HPG)))
    imp = jnp.concatenate([imp, jnp.zeros((KVH * nq, 128 - N_CMP), F32)], axis=1)
    imp2 = imp + pltpu.roll(imp, 127, 1)
    lane = lax.broadcasted_iota(jnp.int32, (KVH * nq, 128), 1)
    blk = lane // 2
    t_sel = past + lax.broadcasted_iota(jnp.int32, (KVH * nq, 128), 0) % nq
    cur = t_sel // L_SEL
    forced = (blk == 0) | (blk == cur) | (blk == cur - 1)
    valid = (blk * L_SEL) <= t_sel
    score = jnp.where(valid, imp2 + jnp.where(forced, FORCE_BONUS, 0.0), -FORCE_BONUS)
    score = jnp.where(((lane & 1) == 0) & (blk < ns_real), score, -jnp.inf)
    cnt = jnp.zeros((KVH * nq, 128), jnp.int32)
    for k in range(ns_real):
        ck = score[:, 2 * k:2 * k + 1]
        beats = (ck > score) | ((ck == score) & (lane > 2 * k))
        cnt = cnt + beats.astype(jnp.int32)
    sel = (cnt < N_SEL).astype(BF16)
    sel_rows = per_group(lambda g: jnp.concatenate([sel[g * nq:(g + 1) * nq]] * HPG, axis=0))
    picked = _dot(sel_rows, expand_ref[...]) > 0.5
    s_p = per_group(lambda g: _dot(qg[g], kt[g]))
    s_n = per_group(lambda g: _dot_nt(qg[g], new_rows(kvn[:, 2 * KV_COLS + g * HD:2 * KV_COLS + (g + 1) * HD])))
    (e_p, e_n), l_s = _softmax_rows([(s_p, picked[:, 0:past]), (s_n, picked[:, past:past + new_pad] & causal_new)])
    o_s = per_group(lambda g: _dot_nt(grp(e_p, g).astype(BF16), vt[g]) + _dot(
        grp(e_n, g).astype(BF16), new_rows(kvn[:, 3 * KV_COLS + g * HD:3 * KV_COLS + (g + 1) * HD]))) / l_s
    s_wp = per_group(lambda g: _dot(qg[g], sw_ref[r, g * HD:(g + 1) * HD, :].astype(BF16)))
    s_wn = per_group(lambda g: _dot_nt(qg[g], new_rows(wn[:, g * HD:(g + 1) * HD])))
    (e_wp, e_wn), l_w = _softmax_rows([(s_wp, ik(wb) > iq(wb) + (wb - WINDOW)), (s_wn, causal_new)])
    o_w = per_group(lambda g: _dot_nt(
        grp(e_wp, g).astype(BF16), sw_ref[r, KV_COLS + g * HD:KV_COLS + (g + 1) * HD, :].astype(BF16)) + _dot(
        grp(e_wn, g).astype(BF16), new_rows(wn[:, KV_COLS + g * HD:KV_COLS + (g + 1) * HD]))) / l_w
    gate = lambda br: jnp.concatenate([gates[:, 3 * hd + br:3 * hd + br + 1] for hd in range(H_A)], axis=0)
    o = gate(0) * o_c + gate(1) * o_s + gate(2) * o_w
    for pair in range(H_A // 2):
        o_ref[r, :, 2 * pair * HD:(2 * pair + 2) * HD] = jnp.concatenate(
            [o[2 * pair * nq:(2 * pair + 1) * nq], o[(2 * pair + 1) * nq:(2 * pair + 2) * nq]], axis=1)


def _attn_sample(page_ids, win_off, q3, kv3, win3, gates3, state_win_t, cache_t, pe4, w4, cosc, sinc, expand, perm,
                 wout_prev):
    nb, nq, _ = q3.shape
    n_prev = 0 if wout_prev is None else wout_prev.shape[0]
    n_pages = page_ids.shape[1]
    page = cache_t.shape[2]
    past = n_pages * page
    wb = state_win_t.shape[2]
    ns = SAMPLE_SEQS
    wout_spec = lambda n: pl.BlockSpec((n, ns, 2 * KV_COLS, wb), lambda b, pt: (0, b, 0, 0))
    seq3 = lambda r, w: pl.BlockSpec((ns, r, w), lambda b, pt: (b, 0, 0))
    page_spec = lambda r, j: pl.BlockSpec((1, 4 * KV_COLS, page), lambda b, pt: (pt[ns * b + r, j], 0, 0))
    const = lambda a: pl.BlockSpec(a.shape, lambda b, pt: (0,) * a.ndim, pipeline_mode=pl.Buffered(1))
    grid_spec = pltpu.PrefetchScalarGridSpec(
        num_scalar_prefetch=1,
        grid=(nb // ns,),
        in_specs=[seq3(nq, Q_COLS), seq3(nq, 4 * KV_COLS), seq3(nq, 2 * KV_COLS), seq3(nq, 128),
                  pl.BlockSpec((ns, 2 * KV_COLS, wb), lambda b, pt: (win_off // ns + b, 0, 0))]
        + [page_spec(r, j) for r in range(ns) for j in range(n_pages)]
        + [const(pe4), const(w4), const(cosc), const(sinc), const(expand), const(perm)]
        + [wout_spec(n_prev)] * bool(n_prev),
        out_specs=[seq3(nq, Q_COLS), wout_spec(n_prev + 1)],
    )
    return pl.pallas_call(
        functools.partial(_attn_sample_kernel, n_pages=n_pages, page=page, past=past),
        grid_spec=grid_spec,
        out_shape=[jax.ShapeDtypeStruct((nb, nq, Q_COLS), F32),
                   jax.ShapeDtypeStruct((n_prev + 1, nb, 2 * KV_COLS, wb), F32)],
        compiler_params=_params(("arbitrary",)),
        name="attn_sample",
    )(page_ids, q3, kv3, win3, gates3, state_win_t, *([cache_t] * (ns * n_pages)), pe4, w4, cosc, sinc, expand, perm,
      *([wout_prev] * bool(n_prev)))


def _conv_kernel(ue_ref, w_ref, b_ref, g_ref, bb_ref, c_ref, shift_scr, *, rows):
    nb, st, ch = c_ref.shape
    pad = CONV_HALO - (CONV_W - 1)
    for r0 in range(0, st, rows):
        acc = jnp.zeros((nb, rows, ch), F32)
        for res in range(8):
            taps = range(res, CONV_W, 8)
            span = 8 * (len(taps) - 1) + rows
            shift_scr[res, :, 0:span, :] = ue_ref[:, pl.ds(pad + r0 + res, span), :]
            for m, k in enumerate(taps):
                acc = acc + shift_scr[res, :, 8 * m:8 * m + rows, :] * w_ref[k:k + 1, :].reshape(1, 1, ch)
        y = _layer_norm(acc + b_ref[...].reshape(1, 1, ch), g_ref[...].reshape(1, 1, ch), bb_ref[...].reshape(1, 1, ch))
        c_ref[:, r0:r0 + rows, :] = y * _sigmoid(y)


def _with_history(body, halo):
    def kernel_fn(cur_ref, prev_ref, *rest):
        ue_scr = rest[-1]
        first = pl.program_id(1) == 0
        ue_scr[:, 0:halo, :] = jnp.where(first, 0.0, prev_ref[...])
        ue_scr[:, halo:, :] = cur_ref[...]
        body(ue_scr, *rest[:-1])
    return kernel_fn


def _conv_shift_scratch(nb, rows, ch):
    return pltpu.VMEM((8, nb, 8 * ((CONV_W - 1) // 8) + rows, ch), F32)


def _halo_specs(ts, halo, ch):
    cur = pl.BlockSpec((1, ts, ch), lambda b, i: (b, i, 0))
    prev = pl.BlockSpec((1, halo, ch), lambda b, i: (b, jnp.maximum(i * (ts // halo) - 1, 0), 0))
    return cur, prev


def _conv_module_seq(u3, w, b, g, bb, ts):
    nseq, s, ch = u3.shape
    cur, prev = _halo_specs(ts, CONV_HALO, ch)
    consts = (w, b, g, bb)
    return pl.pallas_call(
        _with_history(functools.partial(_conv_kernel, rows=32), CONV_HALO),
        grid=(nseq, s // ts),
        in_specs=[cur, prev] + [pl.BlockSpec(a.shape, lambda bi, i: (0, 0), pipeline_mode=pl.Buffered(1)) for a in consts],
        out_specs=cur,
        out_shape=jax.ShapeDtypeStruct((nseq, s, ch), F32),
        scratch_shapes=[_conv_shift_scratch(1, 32, ch), pltpu.VMEM((1, ts + CONV_HALO, ch), F32)],
        compiler_params=_params(("parallel", "arbitrary")),
        name="conv_module_seq",
    )(u3, u3, *consts)


def _conv_module(ue_tiles, w, b, g, bb, nb, st):
    n = ue_tiles.shape[0]
    ch = ue_tiles.shape[2]
    blk = lambda r: pl.BlockSpec((nb, r, ch), lambda i: (i, 0, 0))
    return pl.pallas_call(
        functools.partial(_conv_kernel, rows=min(st, 32)),
        grid=(n // nb,),
        in_specs=[blk(st + CONV_HALO)] + [_const_spec(a.shape) for a in (w, b, g, bb)],
        out_specs=blk(st),
        out_shape=jax.ShapeDtypeStruct((n, st, ch), F32),
        scratch_shapes=[_conv_shift_scratch(nb, min(st, 32), ch)],
        compiler_params=_params(("parallel",)),
        name="conv_module",
    )(ue_tiles, w, b, g, bb)


def _pool_kernel(ue_ref, d_ref, *, rows, tile_axis, pos0):
    nb, st, _ = d_ref.shape
    tile = 0 if tile_axis is None else pl.program_id(tile_axis)
    for r0 in range(0, st, rows):
        pos = pos0 + tile * st + r0 + lax.broadcasted_iota(jnp.int32, (1, rows, G_C), 1)
        for g, wdw in enumerate(POOL_WINDOWS):
            cols = slice(g * G_C, (g + 1) * G_C)
            acc = ue_ref[:, pl.ds(POOL_HALO + r0, rows), cols]
            u = acc
            for k in range(1, wdw):
                acc = acc + ue_ref[:, pl.ds(POOL_HALO + r0 - k, rows), cols]
            cnt = jnp.minimum(pos + 1, wdw).astype(F32)
            d_ref[:, r0:r0 + rows, cols] = acc / cnt - u


def _pool_diff_seq(u3, ts):
    nseq, s, ch = u3.shape
    cur, prev = _halo_specs(ts, POOL_HALO, ch)
    return pl.pallas_call(
        _with_history(functools.partial(_pool_kernel, rows=64, tile_axis=1, pos0=0), POOL_HALO),
        grid=(nseq, s // ts),
        in_specs=[cur, prev],
        out_specs=cur,
        out_shape=jax.ShapeDtypeStruct((nseq, s, ch), F32),
        scratch_shapes=[pltpu.VMEM((1, ts + POOL_HALO, ch), F32)],
        compiler_params=_params(("parallel", "arbitrary")),
        name="pool_diff_seq",
    )(u3, u3)


def _pool_diff(ue_tiles, nb, st, pos0):
    n = ue_tiles.shape[0]
    ch = ue_tiles.shape[2]
    blk = lambda r: pl.BlockSpec((nb, r, ch), lambda i: (i, 0, 0))
    return pl.pallas_call(
        functools.partial(_pool_kernel, rows=min(st, 64), tile_axis=None, pos0=pos0),
        grid=(n // nb,),
        in_specs=[blk(st + POOL_HALO)],
        out_specs=blk(st),
        out_shape=jax.ShapeDtypeStruct((n, st, ch), F32),
        compiler_params=_params(("parallel",)),
        name="pool_diff",
    )(ue_tiles)


def _ffn_ln(x1, w1_ref, w3_ref, w2_ref, g2_ref, b2_ref):
    xb = x1.astype(BF16)
    h1 = _dot(xb, w1_ref[...])
    h3 = _dot(xb, w3_ref[...])
    gated = (h1 * _sigmoid(h1) * h3).astype(BF16)
    return _layer_norm(ALPHA * x1 + _dot(gated, w2_ref[...]), g2_ref[...], b2_ref[...])


def _post_a_kernel(x_ref, oa_ref, c_ref, wo1_ref, wo2_ref, g1_ref, b1_ref, w1_ref, w3_ref, w2_ref, g2_ref, b2_ref,
                   wnext_ref, o_ref, unext_ref):
    y = _dot(oa_ref[...].astype(BF16), wo1_ref[...]) + _dot(c_ref[...].astype(BF16), wo2_ref[...])
    x1 = _layer_norm(ALPHA * x_ref[...] + y, g1_ref[...], b1_ref[...])
    x2 = _ffn_ln(x1, w1_ref, w3_ref, w2_ref, g2_ref, b2_ref)
    o_ref[...] = x2
    unext_ref[...] = _dot(x2.astype(BF16), wnext_ref[...])


def _post_c_kernel(x_ref, d_ref, wg_ref, sc_ref, wo_ref, g1_ref, b1_ref, w1_ref, w3_ref, w2_ref, g2_ref, b2_ref,
                   o_ref):
    db = d_ref[...].astype(BF16)
    z = jnp.concatenate([_dot(db[:, g * G_C:(g + 1) * G_C], wg_ref[g]) for g in range(len(POOL_WINDOWS))], axis=1)
    y = _dot((z * sc_ref[...]).astype(BF16), wo_ref[...])
    x1 = _layer_norm(ALPHA * x_ref[...] + y, g1_ref[...], b1_ref[...])
    o_ref[...] = _ffn_ln(x1, w1_ref, w3_ref, w2_ref, g2_ref, b2_ref)


def _post(body, x, acts, consts, tm, n_out=1):
    t = x.shape[0]
    row = lambda a: pl.BlockSpec((tm, a.shape[1]), lambda i: (i, 0))
    outs = pl.pallas_call(
        body,
        grid=(t // tm,),
        in_specs=[row(x)] + [row(a) for a in acts] + [_const_spec(c.shape) for c in consts],
        out_specs=[row(x)] * n_out,
        out_shape=[jax.ShapeDtypeStruct(x.shape, F32)] * n_out,
        compiler_params=_params(("parallel",)),
        name=body.__name__.strip("_"),
    )(x, *acts, *consts)
    return outs[0] if n_out == 1 else outs


def _rope_tables(pos):
    half = HD // 2
    inv = ROPE_THETA ** (-jnp.arange(half, dtype=F32) / half)
    ang = pos.astype(F32)[:, None] * inv[None, :]
    cos, sin = jnp.cos(ang), jnp.sin(ang)
    return jnp.tile(cos, (1, 4)), jnp.tile(jnp.concatenate([-sin, sin], axis=1), (1, 2))


def _with_hist_rows(u3, hist, halo):
    b, _, c = u3.shape
    return jnp.concatenate([jnp.zeros((b, halo - hist.shape[1], c), u3.dtype), hist, u3], axis=1)


def _prep_weights(p):
    n_a = p["w_in_a"].shape[0]
    cuts = np.cumsum([0, Q_COLS, 4 * KV_COLS, 2 * KV_COLS, GATE_COLS, C_B, C_B])
    out = {"a": [], "c": [], "ffn": []}
    for ia in range(n_a):
        w_in = p["w_in_a"][ia]
        wq, wkv, wwin, wgl, wga, wgb = [w_in[:, cuts[k]:cuts[k + 1]].astype(BF16) for k in range(6)]
        wglt = jnp.pad(wgl.T, ((0, GATE_ROWS - GATE_COLS), (0, 0)))
        wglr = jnp.pad(wgl, ((0, 0), (0, 128 - GATE_COLS)))
        wk, wv = p["w_cmp_k"][ia], p["w_cmp_v"][ia]
        blocks = jnp.stack([wk, wk, wv, wv], axis=1)
        w4 = jnp.einsum("pkde,kl->pkdle", blocks, jnp.eye(4, dtype=F32)).reshape(L_CMP, 4 * HD, 4 * HD).astype(BF16)
        pe4 = jnp.concatenate([p["pe_cmp_k"][ia]] * 2 + [p["pe_cmp_v"][ia]] * 2, axis=1)
        wo = p["w_out_a"][ia].astype(BF16)
        out["a"].append(dict(
            inproj_t=(wq, wkv, wwin, wga, wgb, wglt), inproj_r=(wq, wkv, wwin, wga, wgb, wglr), w4=w4, pe4=pe4,
            conv=(jnp.pad(p["conv_w"][ia], ((0, 1), (0, 0))), p["conv_b"][ia][None], p["conv_ln_g"][ia][None],
                  p["conv_ln_b"][ia][None]),
            wo1=wo[:Q_COLS], wo2=wo[Q_COLS:]))
    for ic in range(p["w_in_c"].shape[0]):
        out["c"].append(dict(w_in=p["w_in_c"][ic].astype(BF16), w_grp=p["w_grp_c"][ic].astype(BF16),
                             scale=p["scale_c"][ic][None], w_out=p["w_out_c"][ic].astype(BF16)))
    for l in range(DEPTH):
        out["ffn"].append((p["ln1_g"][l][None], p["ln1_b"][l][None], p["w1"][l].astype(BF16),
                           p["w3"][l].astype(BF16), p["w2"][l].astype(BF16), p["ln2_g"][l][None], p["ln2_b"][l][None]))
    return out


def _trunk(x3, pos0, wb, wts, cache_kv, page_table, state_win, state_conv, state_pool):
    b, s, _ = x3.shape
    t = b * s
    prompt = cache_kv is None
    tm = min(512, t)
    x = x3.reshape(t, D_MODEL)
    cos, sin = _rope_tables(pos0 + jnp.arange(s, dtype=jnp.int32))
    if not prompt:
        cos, sin = jnp.tile(cos, (tm // s, 1)), jnp.tile(sin, (tm // s, 1))
    cosc, sinc = _rope_tables(jnp.arange(N_CMP, dtype=jnp.int32) * L_CMP + (L_CMP - 1))
    ts = 512 if prompt else s
    nb = 1 if prompt else 8
    if not prompt:
        n_a, n_pool, page = cache_kv.shape[:3]
        cache_t = jnp.transpose(cache_kv, (0, 1, 3, 4, 5, 2)).reshape(n_a * n_pool, 4 * KV_COLS, page)
        sw_t = jnp.transpose(state_win, (0, 1, 3, 4, 5, 2)).reshape(n_a * b, 2 * KV_COLS, wb)
        lane_blk = jnp.arange(128)[:, None]
        key_blk = jnp.arange(page_table.shape[1] * page + 128)[None, :] // L_SEL
        expand = ((lane_blk % 2 == 0) & (lane_blk // 2 == key_blk)).astype(BF16)
        out_row = jnp.arange(8 * L_CMP)[:, None]
        position = jnp.arange(8 * L_CMP)[None, :]
        perm = ((out_row // 8 == position % L_CMP) & (out_row % 8 == position // L_CMP)).astype(BF16)
    kvt_all, wout_all, win_new, conv_new, pool_new = None, None, [], [], []
    for l in range(DEPTH):
        ln1_g, ln1_b, w1, w3, w2, ln2_g, ln2_b = wts["ffn"][l]
        if l % 2 == 0:
            a = wts["a"][l // 2]
            q, kv, win, u, gates, kvt_all = _inproj_a(x, a["inproj_t" if prompt else "inproj_r"], cos, sin, tm,
                                                      prompt, kvt_all, s)
            if prompt:
                oa = _attn_prompt(q, kv, win, gates, a["pe4"], a["w4"], cosc, sinc, b, s)
                win_new.append(win.reshape(b, s, 2, KVH, HD)[:, s - wb:])
                hist = None
            else:
                ia = l // 2
                n_pool = cache_kv.shape[1]
                oa, wout_all = _attn_sample(page_table + ia * n_pool, ia * b, q.reshape(b, s, Q_COLS),
                                            kv.reshape(b, s, 4 * KV_COLS), win.reshape(b, s, 2 * KV_COLS),
                                            gates.reshape(b, s, 128), sw_t, cache_t, a["pe4"], a["w4"], cosc, sinc,
                                            expand, perm, wout_all)
                oa = oa.reshape(t, Q_COLS)
                hist = state_conv[ia]
            u3 = u.reshape(b, s, C_B)
            keep = CONV_W - 1
            conv_new.append(u3[:, s - keep:] if hist is None else jnp.concatenate([hist, u3], axis=1)[:, -keep:])
            if prompt:
                c = _conv_module_seq(u3, *a["conv"], ts).reshape(t, C_B)
            else:
                c = _conv_module(_with_hist_rows(u3, hist, CONV_HALO), *a["conv"], nb, s).reshape(t, C_B)
            x, u_next = _post(_post_a_kernel, x, (oa, c), (a["wo1"], a["wo2"], ln1_g, ln1_b, w1, w3, w2, ln2_g, ln2_b,
                                                           wts["c"][l // 2]["w_in"]), tm, n_out=2)
        else:
            cw = wts["c"][l // 2]
            u3 = u_next.reshape(b, s, D_MODEL)
            hist = None if prompt else state_pool[l // 2]
            if prompt:
                d = _pool_diff_seq(u3, ts).reshape(t, D_MODEL)
            else:
                d = _pool_diff(_with_hist_rows(u3, hist, POOL_HALO), nb, s, pos0).reshape(t, D_MODEL)
            pool_new.append(u3[:, s - POOL_HIST:] if hist is None
                            else jnp.concatenate([hist, u3], axis=1)[:, -POOL_HIST:])
            x = _post(_post_c_kernel, x, (d,),
                      (cw["w_grp"], cw["scale"], cw["w_out"], ln1_g, ln1_b, w1, w3, w2, ln2_g, ln2_b), tm)
    kv_out = jnp.moveaxis(kvt_all, 2, 3).reshape(-1, b, s, 4, KVH, HD)
    if prompt:
        win_out = jnp.stack(win_new)
    else:
        win_out = wout_all.reshape(-1, b, 2, KVH, HD, wb).transpose(0, 1, 5, 2, 3, 4)
    return x.reshape(b, s, D_MODEL), kv_out, win_out, jnp.stack(conv_new), jnp.stack(pool_new)


def kernel(x_prompt, x_sample, cache_kv, state_win, state_conv, state_pool, page_table, w_in_a, w_cmp_k, pe_cmp_k, w_cmp_v, pe_cmp_v, conv_w, conv_b, conv_ln_g, conv_ln_b, w_out_a, w_in_c, w_grp_c, scale_c, w_out_c, ln1_g, ln1_b, ln2_g, ln2_b, w1, w3, w2):
    p = dict(w_in_a=w_in_a, w_cmp_k=w_cmp_k, pe_cmp_k=pe_cmp_k, w_cmp_v=w_cmp_v, pe_cmp_v=pe_cmp_v,
             conv_w=conv_w, conv_b=conv_b, conv_ln_g=conv_ln_g, conv_ln_b=conv_ln_b, w_out_a=w_out_a,
             w_in_c=w_in_c, w_grp_c=w_grp_c, scale_c=scale_c, w_out_c=w_out_c,
             ln1_g=ln1_g, ln1_b=ln1_b, ln2_g=ln2_g, ln2_b=ln2_b, w1=w1, w3=w3, w2=w2)
    wts = _prep_weights(p)
    past_len = page_table.shape[1] * cache_kv.shape[2]
    wb = state_win.shape[2]
    y_p, kv_p, win_p, conv_p, pool_p = _trunk(x_prompt, 0, wb, wts, None, None, None, None, None)
    y_s, kv_s, win_s, conv_s, pool_s = _trunk(x_sample, past_len, wb, wts, cache_kv, page_table, state_win,
                                              state_conv, state_pool)
    return (y_p, y_s, kv_p, kv_s, win_p, win_s, conv_p, conv_s, pool_p, pool_s)
```
